```python
import jax, jax.numpy as jnp
from jax import lax
import numpy as np

D_MODEL = 1024
BATCH = 4
SEQ = 4096
DEPTH = 4
DEC_BATCH = 128
DEC_SEQ = 1
PAST_LEN = 2048
PAGE_SIZE = 128

HEAD_DIM = 64
MIX_WIDTH = D_MODEL
CONV_DIM = MIX_WIDTH // 2
ATTN_DIM = MIX_WIDTH - CONV_DIM
N_HEADS = ATTN_DIM // HEAD_DIM
N_KV_HEADS = 2
GROUP = N_HEADS // N_KV_HEADS
CONV_WIDTH = 3
CMP_STRIDE = 16
CMP_LEN = 2 * CMP_STRIDE
SEL_BLOCK = 64
N_SEL = 16
WINDOW = 512
Q_BLOCK = 128
N_PAGED = 4
N_KV_SLOTS = N_PAGED + 2
KV_COLS = N_KV_SLOTS * N_KV_HEADS * HEAD_DIM
GATE_COLS = 3 * N_HEADS
IN_COLS = 3 * CONV_DIM + N_HEADS * HEAD_DIM + KV_COLS + GATE_COLS
D_FF = -(-8 * D_MODEL // (3 * 256)) * 256
RMS_EPS = 1e-6
NEG_INF = -1e30
FORCE_SCORE = 1e4
SCALE = HEAD_DIM ** -0.5

kernel_name = "hymba_conv_nsa_decoder_step"


def rmsnorm(x, g):
    xf = x.astype(jnp.float32)
    y = xf * lax.rsqrt(jnp.mean(xf * xf, axis=-1, keepdims=True) + RMS_EPS)
    return (y * g.astype(jnp.float32)).astype(x.dtype)


def masked_softmax(s, mask):
    p = jax.nn.softmax(jnp.where(mask, s.astype(jnp.float32), NEG_INF), axis=-1)
    return jnp.where(mask, p, 0.0)


def pad_seq(x, mult):
    t = x.shape[1]
    tp = -(-t // mult) * mult
    return jnp.pad(x, [(0, 0), (0, tp - t)] + [(0, 0)] * (x.ndim - 2))


def short_conv(u, prev, w):
    s = u.shape[1]
    up = jnp.concatenate([prev, u], axis=1)
    y = w[0] * up[:, 0:s]
    for j in range(1, CONV_WIDTH):
        y = y + w[j] * up[:, j:j + s]
    return y, up[:, s:]


def compress(kv, pe, w1, w2):
    b, tp = kv.shape[:2]
    ch = kv.reshape(b, tp // CMP_STRIDE, CMP_STRIDE, N_KV_HEADS, HEAD_DIM)
    blk = jnp.concatenate([ch[:, :-1], ch[:, 1:]], axis=2)
    h = jax.nn.gelu(jnp.einsum('bnjkd,jde->bnke', blk + pe[:, None, :], w1))
    return jnp.einsum('bnke,ef->bnkf', h, w2)


def nsa_context(kv4, pe_k, w1_k, w2_k, pe_v, w1_v, w2_v):
    kvp = pad_seq(kv4, SEL_BLOCK)
    b, tp = kvp.shape[:2]
    kc = compress(kvp[:, :, 0], pe_k, w1_k, w2_k)
    vc = compress(kvp[:, :, 1], pe_v, w1_v, w2_v)
    ks = kvp[:, :, 2].reshape(b, tp // SEL_BLOCK, SEL_BLOCK, N_KV_HEADS, HEAD_DIM)
    vs = kvp[:, :, 3].reshape(b, tp // SEL_BLOCK, SEL_BLOCK, N_KV_HEADS, HEAD_DIM)
    return kc, vc, ks, vs


def cmp_to_sel(nc, ns):
    n = jnp.arange(nc)[:, None]
    s = jnp.arange(ns)[None, :]
    ov = (n * CMP_STRIDE <= s * SEL_BLOCK + SEL_BLOCK - 1) & (n * CMP_STRIDE + CMP_LEN - 1 >= s * SEL_BLOCK)
    return ov.astype(jnp.float32)


def cmp_slc_branch(q, qpos, kc, vc, ks, vs):
    b, sq = q.shape[:2]
    nc, ns = kc.shape[1], ks.shape[1]
    qg = q.reshape(b, sq, N_KV_HEADS, GROUP, HEAD_DIM)
    s_c = jnp.einsum('bqkgd,bnkd->bkgqn', qg, kc) * SCALE
    c_end = jnp.arange(nc) * CMP_STRIDE + CMP_LEN - 1
    p_c = masked_softmax(s_c, c_end[None, :] <= qpos[:, None])
    o_c = jnp.einsum('bkgqn,bnkd->bqkgd', p_c.astype(vc.dtype), vc)
    imp = jnp.einsum('bkgqn,ns->bkqs', p_c, cmp_to_sel(nc, ns))
    s_idx = jnp.arange(ns)[None, :]
    q_blk = (qpos // SEL_BLOCK)[:, None]
    forced = (s_idx == 0) | (s_idx == q_blk) | (s_idx == q_blk - 1)
    imp = jnp.where(s_idx <= q_blk, jnp.where(forced, FORCE_SCORE, imp), -1.0)
    _, sel = lax.top_k(imp, min(N_SEL, ns))
    n_sel = sel.shape[-1]
    bi = jnp.arange(b)[:, None, None, None]
    ki = jnp.arange(N_KV_HEADS)[None, :, None, None]
    kg = ks.transpose(0, 3, 1, 2, 4)[bi, ki, sel]
    vg = vs.transpose(0, 3, 1, 2, 4)[bi, ki, sel]
    kpos = sel[..., None] * SEL_BLOCK + jnp.arange(SEL_BLOCK)
    s_s = jnp.einsum('bqkgd,bkqnrd->bkgqnr', qg, kg) * SCALE
    s_s = s_s.reshape(b, N_KV_HEADS, GROUP, sq, n_sel * SEL_BLOCK)
    mask = (kpos <= qpos[:, None, None]).reshape(b, N_KV_HEADS, 1, sq, n_sel * SEL_BLOCK)
    p_s = masked_softmax(s_s, mask).reshape(b, N_KV_HEADS, GROUP, sq, n_sel, SEL_BLOCK)
    o_s = jnp.einsum('bkgqnr,bkqnrd->bqkgd', p_s.astype(vg.dtype), vg)
    return (o_c.reshape(b, sq, N_HEADS, HEAD_DIM), o_s.reshape(b, sq, N_HEADS, HEAD_DIM))


def window_prompt(q, kw, vw):
    b, s = q.shape[:2]
    nb, nw = s // Q_BLOCK, WINDOW // Q_BLOCK

    def band(x):
        xp = jnp.pad(x, [(0, 0), (WINDOW, 0), (0, 0), (0, 0)])
        xp = xp.reshape(b, nb + nw, Q_BLOCK, N_KV_HEADS, HEAD_DIM)
        return jnp.concatenate([xp[:, i:i + nb] for i in range(nw + 1)], axis=2)

    kb, vb = band(kw), band(vw)
    qb = q.reshape(b, nb, Q_BLOCK, N_KV_HEADS, GROUP, HEAD_DIM)
    qpos = jnp.arange(nb)[:, None] * Q_BLOCK + jnp.arange(Q_BLOCK)
    kpos = (jnp.arange(nb)[:, None] - nw) * Q_BLOCK + jnp.arange((nw + 1) * Q_BLOCK)
    diff = qpos[:, :, None] - kpos[:, None, :]
    mask = (diff >= 0) & (diff < WINDOW) & (kpos[:, None, :] >= 0)
    sc = jnp.einsum('bnqkgd,bnjkd->bnkgqj', qb, kb) * SCALE
    p = masked_softmax(sc, mask[None, :, None, None])
    o = jnp.einsum('bnkgqj,bnjkd->bnqkgd', p.astype(vb.dtype), vb)
    return o.reshape(b, s, N_HEADS, HEAD_DIM)


def window_sample(q, kw, vw, qpos, kpos):
    b, sq = q.shape[:2]
    qg = q.reshape(b, sq, N_KV_HEADS, GROUP, HEAD_DIM)
    sc = jnp.einsum('bqkgd,bjkd->bkgqj', qg, kw) * SCALE
    diff = qpos[:, None] - kpos[None, :]
    p = masked_softmax(sc, (diff >= 0) & (diff < WINDOW))
    o = jnp.einsum('bkgqj,bjkd->bqkgd', p.astype(vw.dtype), vw)
    return o.reshape(b, sq, N_HEADS, HEAD_DIM)


def pre_mix(x, g_mix, w_in):
    b, s = x.shape[:2]
    p = rmsnorm(x, g_mix) @ w_in
    c3, qd = 3 * CONV_DIM, N_HEADS * HEAD_DIM
    bg, cg, hc = p[..., :CONV_DIM], p[..., CONV_DIM:2 * CONV_DIM], p[..., 2 * CONV_DIM:c3]
    q = p[..., c3:c3 + qd].reshape(b, s, N_HEADS, HEAD_DIM)
    kv = p[..., c3 + qd:c3 + qd + KV_COLS].reshape(b, s, N_KV_SLOTS, N_KV_HEADS, HEAD_DIM)
    gates = jax.nn.sigmoid(p[..., c3 + qd + KV_COLS:].astype(jnp.float32)).astype(x.dtype)
    return bg, cg * hc, q, kv, gates.reshape(b, s, 3, N_HEADS)


def combine_nsa(gates, o_c, o_s, o_w):
    b, s = o_c.shape[:2]
    o = gates[:, :, 0, :, None] * o_c + gates[:, :, 1, :, None] * o_s + gates[:, :, 2, :, None] * o_w
    return o.reshape(b, s, ATTN_DIM)


def post_mix(x, yc, ya, g_conv_out, g_attn_out, w_out, g_ffn, w_gate, w_up, w_down):
    mix = jnp.concatenate([rmsnorm(yc, g_conv_out), rmsnorm(ya, g_attn_out)], axis=-1)
    x = x + mix @ w_out
    h = rmsnorm(x, g_ffn)
    return x + (jax.nn.silu(h @ w_gate) * (h @ w_up)) @ w_down


def setup_inputs(seed: int = 0) -> dict:
    key = jax.random.key(seed)
    ks = jax.random.split(key, 24)
    f32 = jnp.float32
    n_pages = PAST_LEN // PAGE_SIZE
    n_used = DEC_BATCH * n_pages
    n_pool = n_used + n_used // 4 + 1
    w_buf = min(WINDOW, PAST_LEN)

    def nrm(k, shape, scale):
        return jax.random.normal(k, shape, f32) * scale

    def gain(k, shape):
        return 1.0 + 0.05 * jax.random.normal(k, shape, f32)

    page_table = jax.random.permutation(ks[5], n_pool)[:n_used].reshape(DEC_BATCH, n_pages).astype(jnp.int32)
    return {
        "x_prompt": nrm(ks[0], (BATCH, SEQ, D_MODEL), 1.0),
        "x_sample": nrm(ks[1], (DEC_BATCH, DEC_SEQ, D_MODEL), 1.0),
        "cache_kv": nrm(ks[2], (DEPTH, n_pool, PAGE_SIZE, N_PAGED, N_KV_HEADS, HEAD_DIM), 1.0),
        "cache_win": nrm(ks[3], (DEPTH, DEC_BATCH, w_buf, 2, N_KV_HEADS, HEAD_DIM), 1.0),
        "state_conv": nrm(ks[4], (DEPTH, DEC_BATCH, CONV_WIDTH - 1, CONV_DIM), 1.0),
        "page_table": page_table,
        "g_mix": gain(ks[6], (DEPTH, D_MODEL)),
        "w_in": nrm(ks[7], (DEPTH, D_MODEL, IN_COLS), D_MODEL ** -0.5),
        "conv_w": nrm(ks[8], (DEPTH, CONV_WIDTH, CONV_DIM), CONV_WIDTH ** -0.5),
        "cmp_pe_k": nrm(ks[9], (DEPTH, CMP_LEN, HEAD_DIM), 0.5),
        "cmp_w1_k": nrm(ks[10], (DEPTH, CMP_LEN, HEAD_DIM, HEAD_DIM), (CMP_LEN * HEAD_DIM) ** -0.5),
        "cmp_w2_k": nrm(ks[11], (DEPTH, HEAD_DIM, HEAD_DIM), HEAD_DIM ** -0.5),
        "cmp_pe_v": nrm(ks[12], (DEPTH, CMP_LEN, HEAD_DIM), 0.5),
        "cmp_w1_v": nrm(ks[13], (DEPTH, CMP_LEN, HEAD_DIM, HEAD_DIM), (CMP_LEN * HEAD_DIM) ** -0.5),
        "cmp_w2_v": nrm(ks[14], (DEPTH, HEAD_DIM, HEAD_DIM), HEAD_DIM ** -0.5),
        "g_conv_out": gain(ks[15], (DEPTH, CONV_DIM)),
        "g_attn_out": gain(ks[16], (DEPTH, ATTN_DIM)),
        "w_out": nrm(ks[17], (DEPTH, MIX_WIDTH, D_MODEL), MIX_WIDTH ** -0.5),
        "g_ffn": gain(ks[18], (DEPTH, D_MODEL)),
        "w_gate": nrm(ks[19], (DEPTH, D_MODEL, D_FF), D_MODEL ** -0.5),
        "w_up": nrm(ks[20], (DEPTH, D_MODEL, D_FF), D_MODEL ** -0.5),
        "w_down": nrm(ks[21], (DEPTH, D_FF, D_MODEL), D_FF ** -0.5),
        "g_final": gain(ks[22], (D_MODEL,)),
    }


def reference(x_prompt, x_sample, cache_kv, cache_win, state_conv, page_table, g_mix, w_in, conv_w,
              cmp_pe_k, cmp_w1_k, cmp_w2_k, cmp_pe_v, cmp_w1_v, cmp_w2_v, g_conv_out, g_attn_out,
              w_out, g_ffn, w_gate, w_up, w_down, g_final):
    xp, xs = x_prompt, x_sample
    b, s = xp.shape[:2]
    db, sq = xs.shape[:2]
    n_pages = page_table.shape[1]
    past_len = n_pages * PAGE_SIZE
    w_len = cache_win.shape[2]
    w_keep = min(WINDOW, s)
    nb = s // Q_BLOCK
    kvp_l, winp_l, convp_l, kvs_l, wins_l, convs_l = [], [], [], [], [], []
    for l in range(DEPTH):
        cmp_w = (cmp_pe_k[l], cmp_w1_k[l], cmp_w2_k[l], cmp_pe_v[l], cmp_w1_v[l], cmp_w2_v[l])
        out_w = (g_conv_out[l], g_attn_out[l], w_out[l], g_ffn[l], w_gate[l], w_up[l], w_down[l])
        bg, u, q, kv, gates = pre_mix(xp, g_mix[l], w_in[l])
        yc, conv_new = short_conv(u, jnp.zeros((b, CONV_WIDTH - 1, CONV_DIM), u.dtype), conv_w[l])
        yc = bg * yc
        kc, vc, ksb, vsb = nsa_context(kv[:, :, :N_PAGED], *cmp_w)
        qb = q.reshape(b, nb, Q_BLOCK, N_HEADS, HEAD_DIM).transpose(1, 0, 2, 3, 4)
        qpos = jnp.arange(s).reshape(nb, Q_BLOCK)
        o_c, o_s = lax.map(lambda a: cmp_slc_branch(a[0], a[1], kc, vc, ksb, vsb), (qb, qpos))
        o_c = o_c.transpose(1, 0, 2, 3, 4).reshape(b, s, N_HEADS, HEAD_DIM)
        o_s = o_s.transpose(1, 0, 2, 3, 4).reshape(b, s, N_HEADS, HEAD_DIM)
        o_w = window_prompt(q, kv[:, :, N_PAGED], kv[:, :, N_PAGED + 1])
        xp = post_mix(xp, yc, combine_nsa(gates, o_c, o_s, o_w), *out_w)
        kvp_l.append(kv[:, :, :N_PAGED])
        winp_l.append(kv[:, s - w_keep:, N_PAGED:])
        convp_l.append(conv_new)
        bg, u, q, kv, gates = pre_mix(xs, g_mix[l], w_in[l])
        yc, conv_new = short_conv(u, state_conv[l], conv_w[l])
        yc = bg * yc
        past = cache_kv[l][page_table].reshape(db, past_len, N_PAGED, N_KV_HEADS, HEAD_DIM)
        kc, vc, ksb, vsb = nsa_context(jnp.concatenate([past, kv[:, :, :N_PAGED]], axis=1), *cmp_w)
        qpos = past_len + jnp.arange(sq)
        o_c, o_s = cmp_slc_branch(q, qpos, kc, vc, ksb, vsb)
        wbuf = jnp.concatenate([cache_win[l], kv[:, :, N_PAGED:]], axis=1)
        kpos = past_len - w_len + jnp.arange(w_len + sq)
        o_w = window_sample(q, wbuf[:, :, 0], wbuf[:, :, 1], qpos, kpos)
        xs = post_mix(xs, yc, combine_nsa(gates, o_c, o_s, o_w), *out_w)
        kvs_l.append(kv[:, :, :N_PAGED])
        wins_l.append(wbuf[:, sq:])
        convs_l.append(conv_new)
    y_prompt = rmsnorm(xp, g_final)
    y_sample = rmsnorm(xs, g_final)
    kv_rows_prompt = jnp.stack(kvp_l)
    win_prompt = jnp.stack(winp_l)
    conv_prompt = jnp.stack(convp_l)
    kv_rows_sample = jnp.stack(kvs_l)
    win_sample = jnp.stack(wins_l)
    conv_sample = jnp.stack(convs_l)
    return (y_prompt, y_sample, kv_rows_prompt, win_prompt, conv_prompt, kv_rows_sample, win_sample, conv_sample)
```

```python
import functools

import numpy as np
import jax
import jax.numpy as jnp
from jax import lax
from jax.experimental import pallas as pl
from jax.experimental.pallas import tpu as pltpu

HEAD_DIM = 64
N_KV_HEADS = 2
GROUP = 4
N_HEADS = N_KV_HEADS * GROUP
LANES = 2 * HEAD_DIM
CONV_DIM = 512
ATTN_DIM = N_HEADS * HEAD_DIM
CONV_WIDTH = 3
CMP_STRIDE = 16
CMP_LEN = 2 * CMP_STRIDE
SEL_BLOCK = 64
N_SEL = 16
WINDOW = 512
Q_BLOCK = 128
PAGE_SIZE = 128
N_PAGED = 4
RMS_EPS = 1e-6
NEG_INF = -1e30
FORCE_SCORE = 1e4
SCALE = HEAD_DIM ** -0.5
GATE_PAD = 128
KEY_CHUNK = 256
VMEM_LIMIT = 56 * 1024 * 1024

F32 = jnp.float32
BF16 = jnp.bfloat16


def _rms(x, g):
    return x * lax.rsqrt(jnp.mean(x * x, axis=-1, keepdims=True) + RMS_EPS) * g


def _dot(a, b):
    return jnp.dot(a, b, preferred_element_type=F32)


def _dot_nt(a, b):
    return lax.dot_general(a, b, (((1,), (1,)), ((), ())), preferred_element_type=F32)


def _params(*sem):
    return pltpu.CompilerParams(dimension_semantics=sem, vmem_limit_bytes=VMEM_LIMIT)


def _premix_body(seq_mode, tm, x_ref, g_ref, wc_ref, wq_ref, wkv_ref, wg_ref, cw_ref, gco_ref, *rest):
    if seq_mode:
        ycn_ref, q_ref, kvp_ref, kvw_ref, kvb_ref, gates_ref, tail_ref, ubuf = rest
    else:
        p1_ref, p2_ref, ycn_ref, q_ref, kvp_ref, kvw_ref, kvb_ref, gates_ref, tail_ref = rest
    x = x_ref[...]
    xn = _rms(x, g_ref[...]).astype(BF16)
    pc = _dot(xn, wc_ref[...])
    bg, u = pc[:, :CONV_DIM], pc[:, CONV_DIM:2 * CONV_DIM] * pc[:, 2 * CONV_DIM:]
    if seq_mode:
        @pl.when(pl.program_id(1) == 0)
        def _():
            ubuf[0:8, :] = jnp.zeros((8, CONV_DIM), F32)
        ubuf[8:8 + tm, :] = u
        up1 = ubuf[7:7 + tm, :]
        up2 = ubuf[6:6 + tm, :]
        ubuf[0:8, :] = ubuf[tm:tm + 8, :]
        tail_ref[...] = u[tm - 8:, :]
    else:
        up1, up2 = p1_ref[...], p2_ref[...]
        tail_ref[...] = u
    cw = cw_ref[...]
    yc = bg * (cw[0:1, :] * up2 + cw[1:2, :] * up1 + cw[2:3, :] * u)
    ycn_ref[...] = _rms(yc, gco_ref[...]).astype(BF16)
    q_ref[...] = (_dot(xn, wq_ref[...]) * SCALE).astype(BF16)
    kv = _dot(xn, wkv_ref[...])
    kvp_ref[...] = kv[:, :N_PAGED * LANES]
    kvw_ref[...] = kv[:, N_PAGED * LANES:]
    kvb_ref[...] = kv[:, 2 * LANES:].astype(BF16)
    gates_ref[...] = jax.nn.sigmoid(_dot(xn, wg_ref[...]))


def _premix(x, layer, w, prev=None, tm=512):
    nb, s, d = x.shape
    seq_mode = prev is None
    if seq_mode:
        grid = (nb, s // tm)
        row = lambda c: pl.BlockSpec((None, tm, c), lambda b, i: (b, i, 0))
        tail_spec = pl.BlockSpec((None, 8, CONV_DIM), lambda b, i: (b, 0, 0))
        tail_shape = (nb, 8, CONV_DIM)
        xin = x
        lead = (nb, s)
    else:
        tm = nb
        grid = (1, 1)
        row = lambda c: pl.BlockSpec((tm, c), lambda b, i: (0, 0))
        tail_spec = row(CONV_DIM)
        tail_shape = (nb, CONV_DIM)
        xin = x.reshape(nb, d)
        lead = (nb,)
    wspec = lambda a: pl.BlockSpec((None,) + a.shape[1:], lambda b, i: (layer,) + (0,) * (a.ndim - 1))
    ws = [w["g_mix"], w["w_c"], w["w_q"], w["w_kv"], w["w_g"], w["conv_w"], w["g_conv_out"]]
    in_specs = [row(d)] + [wspec(a) for a in ws]
    args = [xin] + ws
    scratch = []
    if seq_mode:
        scratch = [pltpu.VMEM((tm + 8, CONV_DIM), F32)]
    else:
        in_specs += [row(CONV_DIM), row(CONV_DIM)]
        args += list(prev)
    out_shape = [
        jax.ShapeDtypeStruct(lead + (CONV_DIM,), BF16),
        jax.ShapeDtypeStruct(lead + (ATTN_DIM,), BF16),
        jax.ShapeDtypeStruct(lead + (N_PAGED * LANES,), F32),
        jax.ShapeDtypeStruct(lead + (2 * LANES,), F32),
        jax.ShapeDtypeStruct(lead + (4 * LANES,), BF16),
        jax.ShapeDtypeStruct(lead + (GATE_PAD,), F32),
        jax.ShapeDtypeStruct(tail_shape, F32),
    ]
    out_specs = [row(CONV_DIM), row(ATTN_DIM), row(N_PAGED * LANES), row(2 * LANES), row(4 * LANES),
                 row(GATE_PAD), tail_spec]
    return pl.pallas_call(
        functools.partial(_premix_body, seq_mode, tm),
        grid=grid, in_specs=in_specs, out_specs=out_specs, out_shape=out_shape,
        scratch_shapes=scratch, compiler_params=_params("arbitrary", "arbitrary"),
        name="premix_seq" if seq_mode else "premix_step",
    )(*args)


def _cmp_bias_body(pe_ref, w1_ref, o_ref):
    o_ref[...] = jnp.sum(pe_ref[...] * w1_ref[...], axis=0, keepdims=True)


def _cmp_bias(pe, w1):
    nl = pe.shape[0]
    k = CMP_LEN * HEAD_DIM
    return pl.pallas_call(
        _cmp_bias_body,
        grid=(nl, 2),
        in_specs=[pl.BlockSpec((None, None, k, 1), lambda l, t: (l, t, 0, 0)),
                  pl.BlockSpec((None, None, k, HEAD_DIM), lambda l, t: (l, t, 0, 0))],
        out_specs=pl.BlockSpec((None, None, 1, HEAD_DIM), lambda l, t: (l, t, 0, 0)),
        out_shape=jax.ShapeDtypeStruct((nl, 2, 1, HEAD_DIM), F32),
        compiler_params=_params("arbitrary", "arbitrary"), name="cmp_bias",
    )(pe, w1)


def _cmp_finish(acc, bias, w2, sbuf):
    n = acc.shape[0]
    sbuf[0:n, :] = acc[:, LANES:]
    sbuf[n:n + 8, :] = jnp.zeros((8, LANES), F32)
    h = jax.nn.gelu(acc[:, :LANES] + sbuf[1:n + 1, :] + bias)
    return _dot(h.astype(BF16), w2)


def _compress_body(nch, k_ref, v_ref, wk_ref, wv_ref, bk_ref, bv_ref, w2k_ref, w2v_ref, kc_ref, vc_ref, sbuf):
    for src, w1, b, w2, out in ((k_ref, wk_ref, bk_ref, w2k_ref, kc_ref), (v_ref, wv_ref, bv_ref, w2v_ref, vc_ref)):
        acc = jnp.zeros((nch, 2 * LANES), F32)
        for j in range(CMP_STRIDE):
            acc = acc + _dot(src[pl.ds(j, nch, stride=CMP_STRIDE), :].astype(BF16), w1[j])
        out[...] = _cmp_finish(acc, b[...], w2[...], sbuf).astype(BF16)


def _compress(kvp, layer, w):
    nb, s, _ = kvp.shape
    nch = s // CMP_STRIDE
    wspec = lambda a: pl.BlockSpec((None,) + a.shape[1:], lambda b: (layer,) + (0,) * (a.ndim - 1))
    ws = [w["cmp_w1_k"], w["cmp_w1_v"], w["cmp_b_k"], w["cmp_b_v"], w["cmp_w2_k"], w["cmp_w2_v"]]
    return pl.pallas_call(
        functools.partial(_compress_body, nch),
        grid=(nb,),
        in_specs=[pl.BlockSpec((None, s, LANES), lambda b: (b, 0, 0)),
                  pl.BlockSpec((None, s, LANES), lambda b: (b, 0, 1))] + [wspec(a) for a in ws],
        out_specs=[pl.BlockSpec((None, nch, LANES), lambda b: (b, 0, 0))] * 2,
        out_shape=[jax.ShapeDtypeStruct((nb, nch, LANES), BF16)] * 2,
        scratch_shapes=[pltpu.VMEM((nch + 8, LANES), F32)],
        compiler_params=_params("arbitrary"), name="compress",
    )(kvp, kvp, *ws)


def _half_mask(rows):
    return lax.broadcasted_iota(jnp.int32, (rows, LANES), 1) < HEAD_DIM


def _stack_q(qt):
    n = qt.shape[0]
    lo = _half_mask(n)
    zero = jnp.zeros((n, LANES), qt.dtype)
    parts = []
    for kvh in range(N_KV_HEADS):
        for g in range(GROUP):
            blk = qt[:, g * LANES:(g + 1) * LANES]
            parts.append(jnp.where(lo if kvh == 0 else ~lo, blk, zero))
    return jnp.concatenate(parts, axis=0)


def _unstack_o(o, n):
    lo = _half_mask(n)
    cols = [jnp.where(lo, o[g * n:(g + 1) * n, :], o[(GROUP + g) * n:(GROUP + g + 1) * n, :]) for g in range(GROUP)]
    return jnp.concatenate(cols, axis=1)


def _masked_softmax(s, mask):
    m = jnp.max(jnp.where(mask, s, NEG_INF), axis=-1, keepdims=True)
    e = jnp.where(mask, jnp.exp(s - m), 0.0)
    l = jnp.sum(e, axis=-1, keepdims=True)
    return e / jnp.where(l > 0.0, l, 1.0)


def _combine(gates, expand, o_c, o_s, o_w, g_out):
    gx = jnp.dot(gates, expand, preferred_element_type=F32, precision=lax.Precision.HIGHEST)
    ya = gx[:, :ATTN_DIM] * o_c + gx[:, ATTN_DIM:2 * ATTN_DIM] * o_s + gx[:, 2 * ATTN_DIM:] * o_w
    return _rms(ya, g_out).astype(BF16)


def _cmp_to_sel(nc_rows, ns):
    n = np.arange(nc_rows)[:, None]
    s = np.arange(LANES)[None, :]
    ov = (n * CMP_STRIDE <= s * SEL_BLOCK + SEL_BLOCK - 1) & (n * CMP_STRIDE + CMP_LEN - 1 >= s * SEL_BLOCK) & (s < ns)
    return jnp.asarray(ov.astype(np.float32))


def _gate_expand():
    e = np.zeros((GATE_PAD, 3 * ATTN_DIM), np.float32)
    for br in range(3):
        for kvh in range(N_KV_HEADS):
            for g in range(GROUP):
                c0 = br * ATTN_DIM + g * LANES + kvh * HEAD_DIM
                e[br * N_HEADS + kvh * GROUP + g, c0:c0 + HEAD_DIM] = 1.0
    return jnp.asarray(e)


def _sel_onehot(n_keys):
    t = np.arange(n_keys)[:, None] // SEL_BLOCK
    return jnp.asarray((t == np.arange(LANES)[None, :]).astype(np.float32), dtype=BF16)


def _prompt_attn_body(ns, q_ref, kc_ref, vc_ref, ksl_ref, vsl_ref, kw_ref, vw_ref, oh_ref, ov_ref, gates_ref,
                      ex_ref, go_ref, out_ref):
    i = pl.program_id(1)
    nq = Q_BLOCK
    rows = N_HEADS * nq
    qz = _stack_q(q_ref[...])
    qpos = i * nq + lax.broadcasted_iota(jnp.int32, (rows, 1), 0) % nq

    kc = kc_ref[...]
    nc = kc.shape[0]
    s_c = _dot_nt(qz, kc)
    c_end = lax.broadcasted_iota(jnp.int32, (rows, nc), 1) * CMP_STRIDE + (CMP_LEN - 1)
    p_c = _masked_softmax(s_c, c_end <= qpos)
    o_c = _dot(p_c.astype(BF16), vc_ref[...])

    psum = jnp.concatenate(
        [sum(p_c[(kvh * GROUP + g) * nq:(kvh * GROUP + g + 1) * nq, :] for g in range(GROUP))
         for kvh in range(N_KV_HEADS)], axis=0)
    imp = jnp.dot(psum, ov_ref[...], preferred_element_type=F32, precision=lax.Precision.HIGHEST)
    r2 = N_KV_HEADS * nq
    s_idx = lax.broadcasted_iota(jnp.int32, (r2, LANES), 1)
    q_blk = (i * nq + lax.broadcasted_iota(jnp.int32, (r2, LANES), 0) % nq) // SEL_BLOCK
    forced = (s_idx == 0) | (s_idx == q_blk) | (s_idx == q_blk - 1)
    imp = jnp.where(s_idx <= q_blk, jnp.where(forced, FORCE_SCORE, imp), -1.0)
    imp = jnp.where(s_idx < ns, imp, -2.0)
    imp_t = imp.T
    tile = imp_t[0:ns, :]
    row_id = lax.broadcasted_iota(jnp.int32, (ns, r2), 0)
    cnt = jnp.zeros((ns, r2), jnp.int32)
    for sp in range(ns):
        other = imp_t[sp:sp + 1, :]
        beats = (other > tile) | ((other == tile) & (sp < row_id))
        cnt = cnt + beats.astype(jnp.int32)
    pen_t = jnp.where(cnt < N_SEL, 0.0, NEG_INF)
    if ns < LANES:
        pen_t = jnp.concatenate([pen_t, jnp.zeros((LANES - ns, r2), F32)], axis=0)
    pen = pen_t.T
    pen = jnp.where(s_idx <= q_blk, pen, jnp.where(s_idx < ns, NEG_INF, 0.0)).astype(BF16)
    pen_rows = jnp.concatenate([pen[kvh * nq:(kvh + 1) * nq, :] for kvh in range(N_KV_HEADS) for _ in range(GROUP)],
                               axis=0)
    q_aug = jnp.concatenate([qz, pen_rows], axis=1)

    ones_v = jnp.ones((KEY_CHUNK, LANES), BF16)

    def chunk(c, carry, causal):
        m_old, acc = carry
        k0 = pl.multiple_of(c * KEY_CHUNK, KEY_CHUNK)
        k_aug = jnp.concatenate([ksl_ref[pl.ds(k0, KEY_CHUNK), :], oh_ref[pl.ds(k0, KEY_CHUNK), :]], axis=1)
        s = _dot_nt(q_aug, k_aug)
        if causal:
            kpos = k0 + lax.broadcasted_iota(jnp.int32, (rows, KEY_CHUNK), 1)
            s = jnp.where(kpos <= qpos, s, NEG_INF)
        m_new = jnp.maximum(m_old, jnp.max(s, axis=-1, keepdims=True))
        p = jnp.exp(s - m_new).astype(BF16)
        v_aug = jnp.concatenate([vsl_ref[pl.ds(k0, KEY_CHUNK), :], ones_v], axis=1)
        acc = jnp.exp(m_old - m_new) * acc + _dot(p, v_aug)
        return m_new, acc

    last = (i * nq) // KEY_CHUNK
    carry = (jnp.full((rows, 1), NEG_INF, F32), jnp.zeros((rows, 2 * LANES), F32))
    carry = lax.fori_loop(0, last, lambda c, cr: chunk(c, cr, False), carry)
    _, acc = chunk(last, carry, True)
    o_s = acc[:, :LANES] / acc[:, LANES:]

    nw = WINDOW + nq
    w0 = pl.multiple_of(jnp.maximum(i * nq - WINDOW, 0), nq)
    s_w = _dot_nt(qz, kw_ref[pl.ds(w0, nw), :])
    diff = qpos - (w0 + lax.broadcasted_iota(jnp.int32, (rows, nw), 1))
    p_w = _masked_softmax(s_w, (diff >= 0) & (diff < WINDOW))
    o_w = _dot(p_w.astype(BF16), vw_ref[pl.ds(w0, nw), :])

    out_ref[...] = _combine(gates_ref[...], ex_ref[...], _unstack_o(o_c, nq), _unstack_o(o_s, nq),
                            _unstack_o(o_w, nq), go_ref[...])


def _prompt_attn(q, kc, vc, kvb, gates, layer, w, consts):
    nb, s, _ = q.shape
    nc = kc.shape[1]
    ns = s // SEL_BLOCK
    col = lambda c: pl.BlockSpec((None, s, LANES), lambda b, i: (b, 0, c))
    whole = lambda a: pl.BlockSpec(a.shape, lambda b, i: (0,) * a.ndim)
    return pl.pallas_call(
        functools.partial(_prompt_attn_body, ns),
        grid=(nb, s // Q_BLOCK),
        in_specs=[pl.BlockSpec((None, Q_BLOCK, ATTN_DIM), lambda b, i: (b, i, 0)),
                  pl.BlockSpec((None, nc, LANES), lambda b, i: (b, 0, 0)),
                  pl.BlockSpec((None, nc, LANES), lambda b, i: (b, 0, 0)),
                  col(0), col(1), col(2), col(3),
                  whole(consts["onehot"]), whole(consts["ov"]),
                  pl.BlockSpec((None, Q_BLOCK, GATE_PAD), lambda b, i: (b, i, 0)),
                  whole(consts["expand"]),
                  pl.BlockSpec((None, 1, ATTN_DIM), lambda b, i: (layer, 0, 0))],
        out_specs=pl.BlockSpec((None, Q_BLOCK, ATTN_DIM), lambda b, i: (b, i, 0)),
        out_shape=jax.ShapeDtypeStruct((nb, s, ATTN_DIM), BF16),
        compiler_params=_params("arbitrary", "arbitrary"), name="prompt_attn",
    )(q, kc, vc, kvb, kvb, kvb, kvb, consts["onehot"], consts["ov"], gates, consts["expand"], w["g_attn_out"])


def _sample_attn_body(n_pages, pt_ref, q_ref, kvp_ref, kvw_ref, gates_ref, *rest):
    pages = rest[:n_pages]
    (win_ref, wk_ref, wv_ref, bk_ref, bv_ref, w2k_ref, w2v_ref, oh_ref, ov_ref, ex_ref, go_ref,
     out_ref, wout_ref, sbuf, cbuf) = rest[n_pages:]
    past = n_pages * PAGE_SIZE
    qpos = past
    nch = past // CMP_STRIDE
    rows = N_HEADS
    qz = _stack_q(q_ref[...])
    kv_new = kvp_ref[...]
    lane = lax.broadcasted_iota(jnp.int32, (rows, LANES), 1)

    def compress(c0, w1, b, w2):
        for p, pg in enumerate(pages):
            cbuf[p * PAGE_SIZE:(p + 1) * PAGE_SIZE, :] = pg[:, c0:c0 + LANES]
        acc = jnp.zeros((nch, 2 * LANES), F32)
        for j in range(CMP_STRIDE):
            acc = acc + _dot(cbuf[pl.ds(j, nch, stride=CMP_STRIDE), :].astype(BF16), w1[j])
        return _cmp_finish(acc, b[...], w2[...], sbuf).astype(BF16)

    kc = compress(0, wk_ref, bk_ref, w2k_ref)
    vc = compress(LANES, wv_ref, bv_ref, w2v_ref)

    s_c = _dot_nt(qz, kc)
    blk = lax.broadcasted_iota(jnp.int32, (rows, nch), 1)
    p_c = _masked_softmax(s_c, (blk * CMP_STRIDE + (CMP_LEN - 1) <= qpos) & (blk < nch - 1))
    o_c = _dot(p_c.astype(BF16), vc)

    ns = (past + 1 + SEL_BLOCK - 1) // SEL_BLOCK
    q_blk = qpos // SEL_BLOCK
    psum = jnp.concatenate([jnp.sum(p_c[kvh * GROUP:(kvh + 1) * GROUP, :], axis=0, keepdims=True)
                            for kvh in range(N_KV_HEADS)] + [jnp.zeros((rows - N_KV_HEADS, nch), F32)], axis=0)
    imp = jnp.dot(psum, ov_ref[...], preferred_element_type=F32, precision=lax.Precision.HIGHEST)
    forced = (lane == 0) | (lane == q_blk) | (lane == q_blk - 1)
    imp = jnp.where(lane <= q_blk, jnp.where(forced, FORCE_SCORE, imp), -1.0)
    imp = jnp.where(lane < ns, imp, -2.0)
    imp_t = imp.T
    sub = lax.broadcasted_iota(jnp.int32, (LANES, LANES), 0)
    lan = lax.broadcasted_iota(jnp.int32, (LANES, LANES), 1)
    pens = []
    for kvh in range(N_KV_HEADS):
        other = imp_t[:, kvh:kvh + 1]
        mine = imp[kvh:kvh + 1, :]
        beats = (other > mine) | ((other == mine) & (sub < lan))
        cnt = jnp.sum(beats.astype(jnp.int32), axis=0, keepdims=True)
        sel = (cnt < N_SEL) & (lane[0:1, :] <= q_blk)
        pens.append(jnp.where(sel | (lane[0:1, :] >= ns), 0.0, NEG_INF))
    pen = jnp.concatenate([pens[kvh] for kvh in range(N_KV_HEADS) for _ in range(GROUP)], axis=0)
    q_aug = jnp.concatenate([qz, pen.astype(BF16)], axis=1)

    s_pages = []
    for p, pg in enumerate(pages):
        k_aug = jnp.concatenate([pg[:, 2 * LANES:3 * LANES].astype(BF16), oh_ref[p * PAGE_SIZE:(p + 1) * PAGE_SIZE, :]],
                                axis=1)
        s_pages.append(_dot_nt(q_aug, k_aug))
    s_s = jnp.concatenate(s_pages, axis=1)
    k_new = kv_new[:, 2 * LANES:3 * LANES].astype(BF16).astype(F32)
    v_new = kv_new[:, 3 * LANES:4 * LANES].astype(BF16).astype(F32)
    pen_new = jnp.sum(jnp.where(lane == q_blk, pen, 0.0), axis=-1, keepdims=True)
    s_new = jnp.sum(qz.astype(F32) * k_new, axis=-1, keepdims=True) + pen_new
    m = jnp.maximum(jnp.max(s_s, axis=-1, keepdims=True), s_new)
    p_s = jnp.exp(s_s - m)
    p_new = jnp.exp(s_new - m)
    den = jnp.sum(p_s, axis=-1, keepdims=True) + p_new
    p_sb = p_s.astype(BF16)
    o_s = p_new.astype(BF16).astype(F32) * v_new
    for p, pg in enumerate(pages):
        o_s = o_s + _dot(p_sb[:, p * PAGE_SIZE:(p + 1) * PAGE_SIZE], pg[:, 3 * LANES:4 * LANES].astype(BF16))
    o_s = o_s / den

    win = win_ref[...]
    w_len = win.shape[0]
    kvw_new = kvw_ref[...]
    s_w = _dot_nt(qz, win[:, :LANES].astype(BF16))
    diff = qpos - (past - w_len + lax.broadcasted_iota(jnp.int32, (rows, w_len), 1))
    ok = (diff >= 0) & (diff < WINDOW)
    s_wn = jnp.sum(qz.astype(F32) * kvw_new[:, :LANES].astype(BF16).astype(F32), axis=-1, keepdims=True)
    m = jnp.maximum(jnp.max(jnp.where(ok, s_w, NEG_INF), axis=-1, keepdims=True), s_wn)
    p_w = jnp.where(ok, jnp.exp(s_w - m), 0.0)
    p_wn = jnp.exp(s_wn - m)
    den = jnp.sum(p_w, axis=-1, keepdims=True) + p_wn
    o_w = (_dot(p_w.astype(BF16), win[:, LANES:].astype(BF16))
           + p_wn.astype(BF16).astype(F32) * kvw_new[:, LANES:].astype(BF16).astype(F32)) / den

    out_ref[...] = _combine(gates_ref[...], ex_ref[...], _unstack_o(o_c, 1), _unstack_o(o_s, 1), _unstack_o(o_w, 1),
                            go_ref[...])
    wout_ref[0:w_len - 1, :] = win_ref[1:w_len, :]
    wout_ref[w_len - 1:w_len, :] = kvw_new


def _sample_attn(q, kvp, kvw, gates, cache_kv, cache_win, page_table, layer, w, consts):
    db, n_pages = page_table.shape
    w_len = cache_win.shape[2]
    nch = n_pages * PAGE_SIZE // CMP_STRIDE
    r3 = lambda a: a.reshape(db, 1, a.shape[-1])
    rowspec = lambda c: pl.BlockSpec((None, 1, c), lambda b, pt: (b, 0, 0))
    whole = lambda a: pl.BlockSpec(a.shape, lambda b, pt: (0,) * a.ndim)
    wspec = lambda a: pl.BlockSpec((None,) + a.shape[1:], lambda b, pt: (layer,) + (0,) * (a.ndim - 1))
    page_specs = [pl.BlockSpec((None, None, PAGE_SIZE, N_PAGED * LANES),
                               functools.partial(lambda p, b, pt: (layer, pt[b, p], 0, 0), p)) for p in range(n_pages)]
    ws = [w["cmp_w1_k"], w["cmp_w1_v"], w["cmp_b_k"], w["cmp_b_v"], w["cmp_w2_k"], w["cmp_w2_v"]]
    grid_spec = pltpu.PrefetchScalarGridSpec(
        num_scalar_prefetch=1, grid=(db,),
        in_specs=[rowspec(ATTN_DIM), rowspec(N_PAGED * LANES), rowspec(2 * LANES), rowspec(GATE_PAD)] + page_specs
        + [pl.BlockSpec((None, None, w_len, 2 * LANES), lambda b, pt: (layer, b, 0, 0))]
        + [wspec(a) for a in ws]
        + [whole(consts["onehot_s"]), whole(consts["ov_s"]), whole(consts["expand"]), wspec(w["g_attn_out"])],
        out_specs=[rowspec(ATTN_DIM), pl.BlockSpec((None, w_len, 2 * LANES), lambda b, pt: (b, 0, 0))],
        scratch_shapes=[pltpu.VMEM((nch + 8, LANES), F32), pltpu.VMEM((n_pages * PAGE_SIZE, LANES), F32)],
    )
    out, wout = pl.pallas_call(
        functools.partial(_sample_attn_body, n_pages),
        grid_spec=grid_spec,
        out_shape=[jax.ShapeDtypeStruct((db, 1, ATTN_DIM), BF16), jax.ShapeDtypeStruct((db, w_len, 2 * LANES), F32)],
        compiler_params=_params("arbitrary"), name="sample_attn",
    )(page_table, r3(q), r3(kvp), r3(kvw), r3(gates), *([cache_kv] * n_pages), cache_win, *ws,
      consts["onehot_s"], consts["ov_s"], consts["expand"], w["g_attn_out"])
    return out.reshape(db, ATTN_DIM), wout


def _postmix_body(final, x_ref, yc_ref, ya_ref, woc_ref, woa_ref, gf_ref, wg_ref, wu_ref, wd_ref, gl_ref, xo_ref, *rest):
    h_s, acc_s = rest[-2:]
    f = pl.program_id(1)

    @pl.when(f == 0)
    def _():
        x1 = x_ref[...] + _dot(yc_ref[...], woc_ref[...]) + _dot(ya_ref[...], woa_ref[...])
        h_s[...] = _rms(x1, gf_ref[...]).astype(BF16)
        acc_s[...] = x1

    h = h_s[...]
    a = _dot(h, wg_ref[...])
    act = (a * jax.nn.sigmoid(a) * _dot(h, wu_ref[...])).astype(BF16)
    acc_s[...] += _dot(act, wd_ref[...])

    @pl.when(f == pl.num_programs(1) - 1)
    def _():
        out = acc_s[...]
        xo_ref[...] = out
        if final:
            rest[0][...] = _rms(out, gl_ref[...])


def _postmix(x, ycn, yan, layer, w, final, tm=512, n_ff=2):
    t, d = x.shape
    tm = min(tm, t)
    d_ff = w["w_gate"].shape[-1]
    tf = d_ff // n_ff
    assert tf * n_ff == d_ff and tf % LANES == 0
    row = lambda c: pl.BlockSpec((tm, c), lambda i, f: (i, 0))
    wspec = lambda a: pl.BlockSpec((None,) + a.shape[1:], lambda i, f: (layer,) + (0,) * (a.ndim - 1))
    gl = w["g_final"]
    n_out = 2 if final else 1
    outs = pl.pallas_call(
        functools.partial(_postmix_body, final),
        grid=(t // tm, n_ff),
        in_specs=[row(d), row(CONV_DIM), row(ATTN_DIM), wspec(w["w_out_c"]), wspec(w["w_out_a"]), wspec(w["g_ffn"]),
                  pl.BlockSpec((None, d, tf), lambda i, f: (layer, 0, f)),
                  pl.BlockSpec((None, d, tf), lambda i, f: (layer, 0, f)),
                  pl.BlockSpec((None, tf, d), lambda i, f: (layer, f, 0)),
                  pl.BlockSpec(gl.shape, lambda i, f: (0, 0))],
        out_specs=[row(d)] * n_out,
        out_shape=[jax.ShapeDtypeStruct((t, d), F32)] * n_out,
        scratch_shapes=[pltpu.VMEM((tm, d), BF16), pltpu.VMEM((tm, d), F32)],
        compiler_params=_params("arbitrary", "arbitrary"), name="postmix",
    )(x, ycn, yan, w["w_out_c"], w["w_out_a"], w["g_ffn"], w["w_gate"], w["w_up"], w["w_down"], gl)
    return outs


def _prep_weights(g_mix, w_in, conv_w, cmp_pe_k, cmp_w1_k, cmp_w2_k, cmp_pe_v, cmp_w1_v, cmp_w2_v, g_conv_out,
                  g_attn_out, w_out, g_ffn, w_gate, w_up, w_down, g_final):
    nl, d, _ = w_in.shape
    c3 = 3 * CONV_DIM
    kv_cols = 6 * LANES

    def perm_heads(a, axis):
        shp = a.shape
        a = a.reshape(shp[:axis] + (N_KV_HEADS, GROUP, HEAD_DIM) + shp[axis + 1:])
        a = jnp.swapaxes(a, axis, axis + 1)
        return a.reshape(shp)

    def blockdiag2(a):
        z = jnp.zeros_like(a)
        return jnp.concatenate([jnp.concatenate([a, z], axis=-1), jnp.concatenate([z, a], axis=-1)], axis=-2)

    def cmp_w1(w1):
        a = blockdiag2(w1[:, :CMP_STRIDE])
        b = blockdiag2(w1[:, CMP_STRIDE:])
        return jnp.concatenate([a, b], axis=-1).astype(BF16)

    pe = jnp.stack([cmp_pe_k, cmp_pe_v], axis=1).reshape(nl, 2, CMP_LEN * HEAD_DIM, 1)
    w1 = jnp.stack([cmp_w1_k, cmp_w1_v], axis=1).reshape(nl, 2, CMP_LEN * HEAD_DIM, HEAD_DIM)
    bias = _cmp_bias(pe, w1)
    bias2 = jnp.concatenate([bias, bias], axis=-1)
    gates_w = jnp.pad(w_in[:, :, c3 + ATTN_DIM + kv_cols:], ((0, 0), (0, 0), (0, GATE_PAD - 3 * N_HEADS)))
    return {
        "g_mix": g_mix.reshape(nl, 1, d),
        "w_c": w_in[:, :, :c3].astype(BF16),
        "w_q": perm_heads(w_in[:, :, c3:c3 + ATTN_DIM], 2).astype(BF16),
        "w_kv": w_in[:, :, c3 + ATTN_DIM:c3 + ATTN_DIM + kv_cols].astype(BF16),
        "w_g": gates_w.astype(BF16),
        "conv_w": conv_w,
        "g_conv_out": g_conv_out.reshape(nl, 1, CONV_DIM),
        "cmp_w1_k": cmp_w1(cmp_w1_k), "cmp_w1_v": cmp_w1(cmp_w1_v),
        "cmp_b_k": bias2[:, 0], "cmp_b_v": bias2[:, 1],
        "cmp_w2_k": blockdiag2(cmp_w2_k).astype(BF16), "cmp_w2_v": blockdiag2(cmp_w2_v).astype(BF16),
        "g_attn_out": perm_heads(g_attn_out, 1).reshape(nl, 1, ATTN_DIM),
        "w_out_c": w_out[:, :CONV_DIM].astype(BF16),
        "w_out_a": perm_heads(w_out[:, CONV_DIM:], 1).astype(BF16),
        "g_ffn": g_ffn.reshape(nl, 1, d),
        "w_gate": w_gate.astype(BF16), "w_up": w_up.astype(BF16), "w_down": w_down.astype(BF16),
        "g_final": g_final.reshape(1, d),
    }


def kernel(x_prompt, x_sample, cache_kv, cache_win, state_conv, page_table, g_mix, w_in, conv_w, cmp_pe_k, cmp_w1_k, cmp_w2_k, cmp_pe_v, cmp_w1_v, cmp_w2_v, g_conv_out, g_attn_out, w_out, g_ffn, w_gate, w_up, w_down, g_final):
    nl = w_in.shape[0]
    b, s, d = x_prompt.shape
    db, sq = x_sample.shape[:2]
    assert sq == 1 and s % 512 == 0
    n_pages = page_table.shape[1]
    past = n_pages * PAGE_SIZE
    w_len = cache_win.shape[2]
    w_keep = min(WINDOW, s)
    w = _prep_weights(g_mix, w_in, conv_w, cmp_pe_k, cmp_w1_k, cmp_w2_k, cmp_pe_v, cmp_w1_v, cmp_w2_v, g_conv_out,
                      g_attn_out, w_out, g_ffn, w_gate, w_up, w_down, g_final)
    consts = {
        "expand": _gate_expand(),
        "onehot": _sel_onehot(s), "ov": _cmp_to_sel(s // CMP_STRIDE, s // SEL_BLOCK),
        "onehot_s": _sel_onehot(past), "ov_s": _cmp_to_sel(past // CMP_STRIDE, (past + 1 + SEL_BLOCK - 1) // SEL_BLOCK),
    }
    cache_kv = cache_kv.reshape(nl, cache_kv.shape[1], PAGE_SIZE, N_PAGED * LANES)
    cache_win = cache_win.reshape(nl, db, w_len, 2 * LANES)

    xp, xs = x_prompt, x_sample
    kvp_l, winp_l, convp_l, kvs_l, wins_l, convs_l = [], [], [], [], [], []
    for l in range(nl):
        final = l == nl - 1
        ycn, q, kvp, kvw, kvb, gates, tail = _premix(xp, l, w)
        kc, vc = _compress(kvp, l, w)
        yan = _prompt_attn(q, kc, vc, kvb, gates, l, w, consts)
        outs = _postmix(xp.reshape(b * s, d), ycn.reshape(b * s, -1), yan.reshape(b * s, -1), l, w, final)
        xp = outs[0].reshape(b, s, d)
        if final:
            y_prompt = outs[1].reshape(b, s, d)
        kvp_l.append(kvp)
        winp_l.append(kvw[:, s - w_keep:])
        convp_l.append(tail[:, 8 - (CONV_WIDTH - 1):])
        ycn, q, kvp, kvw, kvb, gates, u = _premix(xs, l, w, prev=(state_conv[l, :, 1], state_conv[l, :, 0]))
        yan, wout = _sample_attn(q, kvp, kvw, gates, cache_kv, cache_win, page_table, l, w, consts)
        outs = _postmix(xs.reshape(db, d), ycn, yan, l, w, final)
        xs = outs[0].reshape(db, 1, d)
        if final:
            y_sample = outs[1].reshape(db, 1, d)
        kvs_l.append(kvp)
        wins_l.append(wout)
        convs_l.append(jnp.stack([state_conv[l, :, 1], u], axis=1))
    kv_shape = (N_PAGED, N_KV_HEADS, HEAD_DIM)
    return (y_prompt, y_sample,
            jnp.stack(kvp_l).reshape((nl, b, s) + kv_shape),
            jnp.stack(winp_l).reshape((nl, b, w_keep, 2) + kv_shape[1:]),
            jnp.stack(convp_l),
            jnp.stack(kvs_l).reshape((nl, db, 1) + kv_shape),
            jnp.stack(wins_l).reshape((nl, db, w_len, 2) + kv_shape[1:]),
            jnp.stack(convs_l))
```

```python
import functools

import numpy as np
import jax
import jax.numpy as jnp
from jax import lax
from jax.experimental import pallas as pl
from jax.experimental.pallas import tpu as pltpu

HEAD_DIM = 64
N_KV_HEADS = 2
GROUP = 4
N_HEADS = N_KV_HEADS * GROUP
LANES = 2 * HEAD_DIM
CONV_DIM = 512
ATTN_DIM = N_HEADS * HEAD_DIM
CONV_WIDTH = 3
CMP_STRIDE = 16
CMP_LEN = 2 * CMP_STRIDE
SEL_BLOCK = 64
N_SEL = 16
WINDOW = 512
Q_BLOCK = 128
PAGE_SIZE = 128
N_PAGED = 4
RMS_EPS = 1e-6
NEG_INF = -1e30
FORCE_SCORE = 1e4
SCALE = HEAD_DIM ** -0.5
Q_SCALE = SCALE * 1.4426950408889634
GATE_PAD = 128
KEY_CHUNK = 512
VMEM_LIMIT = 56 * 1024 * 1024

F32 = jnp.float32
BF16 = jnp.bfloat16


def _rms(x, g):
    return x * lax.rsqrt(jnp.mean(x * x, axis=-1, keepdims=True) + RMS_EPS) * g


def _dot(a, b):
    return jnp.dot(a, b, preferred_element_type=F32)


def _dot_nt(a, b):
    return lax.dot_general(a, b, (((1,), (1,)), ((), ())), preferred_element_type=F32)


def _params(*sem):
    return pltpu.CompilerParams(dimension_semantics=sem, vmem_limit_bytes=VMEM_LIMIT)


def _premix_body(seq_mode, tm, x_ref, g_ref, wc_ref, wq_ref, wkv_ref, wg_ref, cw_ref, gco_ref, *rest):
    if seq_mode:
        ycn_ref, q_ref, kvp_ref, kvw_ref, kvb_ref, gates_ref, tail_ref, ubuf = rest
    else:
        p1_ref, p2_ref, ycn_ref, q_ref, kvp_ref, kvw_ref, kvb_ref, gates_ref, tail_ref = rest
    x = x_ref[...]
    xn = _rms(x, g_ref[...]).astype(BF16)
    pc = _dot(xn, wc_ref[...])
    bg, u = pc[:, :CONV_DIM], pc[:, CONV_DIM:2 * CONV_DIM] * pc[:, 2 * CONV_DIM:]
    if seq_mode:
        @pl.when(pl.program_id(1) == 0)
        def _():
            ubuf[0:8, :] = jnp.zeros((8, CONV_DIM), F32)
        ubuf[8:8 + tm, :] = u
        up1 = ubuf[7:7 + tm, :]
        up2 = ubuf[6:6 + tm, :]
        ubuf[0:8, :] = ubuf[tm:tm + 8, :]
        tail_ref[...] = u[tm - 8:, :]
    else:
        up1, up2 = p1_ref[...], p2_ref[...]
        tail_ref[...] = u
    cw = cw_ref[...]
    yc = bg * (cw[0:1, :] * up2 + cw[1:2, :] * up1 + cw[2:3, :] * u)
    ycn_ref[...] = _rms(yc, gco_ref[...]).astype(BF16)
    q_ref[...] = (_dot(xn, wq_ref[...]) * Q_SCALE).astype(BF16)
    kv = _dot(xn, wkv_ref[...])
    kvp_ref[...] = kv[:, :N_PAGED * LANES]
    kvw_ref[...] = kv[:, N_PAGED * LANES:]
    kvb_ref[...] = kv[:, 2 * LANES:].astype(BF16)
    gates_ref[...] = jax.nn.sigmoid(_dot(xn, wg_ref[...]))


def _premix(x, layer, w, prev=None, tm=512):
    nb, s, d = x.shape
    seq_mode = prev is None
    if seq_mode:
        grid = (nb, s // tm)
        row = lambda c: pl.BlockSpec((None, tm, c), lambda b, i: (b, i, 0))
        tail_spec = pl.BlockSpec((None, 8, CONV_DIM), lambda b, i: (b, 0, 0))
        tail_shape = (nb, 8, CONV_DIM)
        xin = x
        lead = (nb, s)
    else:
        tm = nb
        grid = (1, 1)
        row = lambda c: pl.BlockSpec((tm, c), lambda b, i: (0, 0))
        tail_spec = row(CONV_DIM)
        tail_shape = (nb, CONV_DIM)
        xin = x.reshape(nb, d)
        lead = (nb,)
    wspec = lambda a: pl.BlockSpec((None,) + a.shape[1:], lambda b, i: (layer,) + (0,) * (a.ndim - 1))
    ws = [w["g_mix"], w["w_c"], w["w_q"], w["w_kv"], w["w_g"], w["conv_w"], w["g_conv_out"]]
    in_specs = [row(d)] + [wspec(a) for a in ws]
    args = [xin] + ws
    scratch = []
    if seq_mode:
        scratch = [pltpu.VMEM((tm + 8, CONV_DIM), F32)]
    else:
        in_specs += [row(CONV_DIM), row(CONV_DIM)]
        args += list(prev)
    out_shape = [
        jax.ShapeDtypeStruct(lead + (CONV_DIM,), BF16),
        jax.ShapeDtypeStruct(lead + (ATTN_DIM,), BF16),
        jax.ShapeDtypeStruct(lead + (N_PAGED * LANES,), F32),
        jax.ShapeDtypeStruct(lead + (2 * LANES,), F32),
        jax.ShapeDtypeStruct(lead + (4 * LANES,), BF16),
        jax.ShapeDtypeStruct(lead + (GATE_PAD,), F32),
        jax.ShapeDtypeStruct(tail_shape, F32),
    ]
    out_specs = [row(CONV_DIM), row(ATTN_DIM), row(N_PAGED * LANES), row(2 * LANES), row(4 * LANES),
                 row(GATE_PAD), tail_spec]
    return pl.pallas_call(
        functools.partial(_premix_body, seq_mode, tm),
        grid=grid, in_specs=in_specs, out_specs=out_specs, out_shape=out_shape,
        scratch_shapes=scratch, compiler_params=_params("arbitrary", "arbitrary"),
        name="premix_seq" if seq_mode else "premix_step",
    )(*args)


def _cmp_bias_body(pe_ref, w1_ref, o_ref):
    o_ref[...] = jnp.sum(pe_ref[...] * w1_ref[...], axis=0, keepdims=True)


def _cmp_bias(pe, w1):
    nl = pe.shape[0]
    k = CMP_LEN * HEAD_DIM
    return pl.pallas_call(
        _cmp_bias_body,
        grid=(nl, 2),
        in_specs=[pl.BlockSpec((None, None, k, 1), lambda l, t: (l, t, 0, 0)),
                  pl.BlockSpec((None, None, k, HEAD_DIM), lambda l, t: (l, t, 0, 0))],
        out_specs=pl.BlockSpec((None, None, 1, HEAD_DIM), lambda l, t: (l, t, 0, 0)),
        out_shape=jax.ShapeDtypeStruct((nl, 2, 1, HEAD_DIM), F32),
        compiler_params=_params("arbitrary", "arbitrary"), name="cmp_bias",
    )(pe, w1)


def _cmp_finish(acc, bias, w2, sbuf):
    n = acc.shape[0]
    sbuf[0:n, :] = acc[:, LANES:]
    sbuf[n:n + 8, :] = jnp.zeros((8, LANES), F32)
    h = jax.nn.gelu(acc[:, :LANES] + sbuf[1:n + 1, :] + bias)
    return _dot(h.astype(BF16), w2)


def _compress_body(nch, k_ref, v_ref, wk_ref, wv_ref, bk_ref, bv_ref, w2k_ref, w2v_ref, kc_ref, vc_ref, sbuf):
    for src, w1, b, w2, out in ((k_ref, wk_ref, bk_ref, w2k_ref, kc_ref), (v_ref, wv_ref, bv_ref, w2v_ref, vc_ref)):
        acc = jnp.zeros((nch, 2 * LANES), F32)
        for j in range(CMP_STRIDE):
            acc = acc + _dot(src[pl.ds(j, nch, stride=CMP_STRIDE), :].astype(BF16), w1[j])
        out[...] = _cmp_finish(acc, b[...], w2[...], sbuf).astype(BF16)


def _compress(kvp, layer, w):
    nb, s, _ = kvp.shape
    nch = s // CMP_STRIDE
    wspec = lambda a: pl.BlockSpec((None,) + a.shape[1:], lambda b: (layer,) + (0,) * (a.ndim - 1))
    ws = [w["cmp_w1_k"], w["cmp_w1_v"], w["cmp_b_k"], w["cmp_b_v"], w["cmp_w2_k"], w["cmp_w2_v"]]
    return pl.pallas_call(
        functools.partial(_compress_body, nch),
        grid=(nb,),
        in_specs=[pl.BlockSpec((None, s, LANES), lambda b: (b, 0, 0)),
                  pl.BlockSpec((None, s, LANES), lambda b: (b, 0, 1))] + [wspec(a) for a in ws],
        out_specs=[pl.BlockSpec((None, nch, LANES), lambda b: (b, 0, 0))] * 2,
        out_shape=[jax.ShapeDtypeStruct((nb, nch, LANES), BF16)] * 2,
        scratch_shapes=[pltpu.VMEM((nch + 8, LANES), F32)],
        compiler_params=_params("arbitrary"), name="compress",
    )(kvp, kvp, *ws)


def _half_mask(rows):
    return lax.broadcasted_iota(jnp.int32, (rows, LANES), 1) < HEAD_DIM


def _stack_q(qt):
    n = qt.shape[0]
    lo = _half_mask(n)
    zero = jnp.zeros((n, LANES), qt.dtype)
    parts = []
    for kvh in range(N_KV_HEADS):
        for g in range(GROUP):
            blk = qt[:, g * LANES:(g + 1) * LANES]
            parts.append(jnp.where(lo if kvh == 0 else ~lo, blk, zero))
    return jnp.concatenate(parts, axis=0)


def _unstack_o(o, n):
    lo = _half_mask(n)
    cols = [jnp.where(lo, o[g * n:(g + 1) * n, :], o[(GROUP + g) * n:(GROUP + g + 1) * n, :]) for g in range(GROUP)]
    return jnp.concatenate(cols, axis=1)


def _masked_softmax(s, mask):
    m = jnp.max(jnp.where(mask, s, NEG_INF), axis=-1, keepdims=True)
    e = jnp.where(mask, jnp.exp2(s - m), 0.0)
    l = jnp.sum(e, axis=-1, keepdims=True)
    return e / jnp.where(l > 0.0, l, 1.0)


def _add_tile_bias(s, bias):
    n = bias.shape[0]
    return jnp.concatenate([s[r * n:(r + 1) * n, :] + bias for r in range(s.shape[0] // n)], axis=0)


def _split_bf16(x, terms):
    out = []
    for _ in range(terms):
        hi = x.astype(BF16)
        out.append(hi)
        x = x - hi.astype(F32)
    return out


def _dot_exact_rhs(x, rhs_bf16, terms):
    return sum(_dot(t, rhs_bf16) for t in _split_bf16(x, terms))


def _combine(gates, expand, o_c, o_s, o_w, g_out):
    gx = _dot_exact_rhs(gates, expand, 2)
    ya = gx[:, :ATTN_DIM] * o_c + gx[:, ATTN_DIM:2 * ATTN_DIM] * o_s + gx[:, 2 * ATTN_DIM:] * o_w
    return _rms(ya, g_out).astype(BF16)


def _cmp_to_sel(nc_rows, ns):
    n = np.arange(nc_rows)[:, None]
    s = np.arange(LANES)[None, :]
    ov = (n * CMP_STRIDE <= s * SEL_BLOCK + SEL_BLOCK - 1) & (n * CMP_STRIDE + CMP_LEN - 1 >= s * SEL_BLOCK) & (s < ns)
    return jnp.asarray(ov.astype(np.float32), dtype=BF16)


def _gate_expand():
    e = np.zeros((GATE_PAD, 3 * ATTN_DIM), np.float32)
    for br in range(3):
        for kvh in range(N_KV_HEADS):
            for g in range(GROUP):
                c0 = br * ATTN_DIM + g * LANES + kvh * HEAD_DIM
                e[br * N_HEADS + kvh * GROUP + g, c0:c0 + HEAD_DIM] = 1.0
    return jnp.asarray(e, dtype=BF16)


def _sel_onehot(n_keys):
    t = np.arange(n_keys)[:, None] // SEL_BLOCK
    return jnp.asarray((t == np.arange(LANES)[None, :]).astype(np.float32), dtype=BF16)


def _chunk_perm():
    m = np.arange(PAGE_SIZE)[:, None]
    t = np.arange(PAGE_SIZE)[None, :]
    per_page = PAGE_SIZE // CMP_STRIDE
    return jnp.asarray((t == CMP_STRIDE * (m % per_page) + m // per_page).astype(np.float32), dtype=BF16)


def _rank_penalty(imp_t, ns):
    r = imp_t.shape[1]
    sub = lax.broadcasted_iota(jnp.int32, (8, r), 0)
    cnt = jnp.zeros((ns, r), F32)
    for sp in range(ns):
        other = imp_t[sp:sp + 1, :]
        parts = []
        for blk in range(ns // 8):
            t = imp_t[blk * 8:(blk + 1) * 8, :]
            ge = jnp.where(other >= t, 1.0, 0.0)
            gt = jnp.where(other > t, 1.0, 0.0)
            if blk * 8 > sp:
                parts.append(ge)
            elif blk * 8 + 7 <= sp:
                parts.append(gt)
            else:
                parts.append(jnp.where(sub > sp - blk * 8, ge, gt))
        cnt = cnt + jnp.concatenate(parts, axis=0)
    return jnp.where(cnt < N_SEL, 0.0, NEG_INF)


def _prompt_attn_body(ns, q_ref, kc_ref, vc_ref, ksl_ref, vsl_ref, kw_ref, vw_ref, oh_ref, ov_ref, gates_ref,
                      ex_ref, go_ref, out_ref, pen_s):
    i = pl.program_id(1)
    nq = Q_BLOCK
    rows = N_HEADS * nq
    qz = _stack_q(q_ref[...])
    qpos_t = i * nq + lax.broadcasted_iota(jnp.int32, (nq, 1), 0)

    kc = kc_ref[...]
    nc = kc.shape[0]
    c_end = lax.broadcasted_iota(jnp.int32, (nq, nc), 1) * CMP_STRIDE + (CMP_LEN - 1)
    s_c = _add_tile_bias(_dot_nt(qz, kc), jnp.where(c_end <= qpos_t, 0.0, NEG_INF))
    e_c = jnp.exp2(s_c - jnp.max(s_c, axis=-1, keepdims=True))
    any_c = jnp.where(qpos_t >= CMP_LEN - 1, 1.0, 0.0)
    p_c = e_c * (jnp.concatenate([any_c] * N_HEADS, axis=0) / jnp.sum(e_c, axis=-1, keepdims=True))
    o_c = _dot(p_c.astype(BF16), vc_ref[...])

    psum = jnp.concatenate(
        [sum(p_c[(kvh * GROUP + g) * nq:(kvh * GROUP + g + 1) * nq, :] for g in range(GROUP))
         for kvh in range(N_KV_HEADS)], axis=0)
    imp = _dot_exact_rhs(psum, ov_ref[...], 3)
    r2 = N_KV_HEADS * nq
    s_idx = lax.broadcasted_iota(jnp.int32, (r2, LANES), 1)
    q_blk = (i * nq + lax.broadcasted_iota(jnp.int32, (r2, LANES), 0) % nq) // SEL_BLOCK
    forced = (s_idx == 0) | (s_idx == q_blk) | (s_idx == q_blk - 1)
    imp = jnp.where(s_idx <= q_blk, jnp.where(forced, FORCE_SCORE, imp), -1.0)
    imp = jnp.where(s_idx < ns, imp, -2.0)
    pen_s[...] = jnp.zeros((LANES, r2), F32)

    nw = WINDOW + nq
    w0 = pl.multiple_of(jnp.maximum(i * nq - WINDOW, 0), nq)
    diff = qpos_t - (w0 + lax.broadcasted_iota(jnp.int32, (nq, nw), 1))
    s_w = _add_tile_bias(_dot_nt(qz, kw_ref[pl.ds(w0, nw), :]), jnp.where((diff >= 0) & (diff < WINDOW), 0.0, NEG_INF))
    e_w = jnp.exp2(s_w - jnp.max(s_w, axis=-1, keepdims=True))
    o_w = _dot(e_w.astype(BF16), vw_ref[pl.ds(w0, nw), :]) / jnp.sum(e_w, axis=-1, keepdims=True)

    @pl.when((i + 1) * nq > N_SEL * SEL_BLOCK)
    def _():
        pen_s[0:ns, :] = _rank_penalty(imp.T[0:ns, :], ns)

    pen = pen_s[...].T
    pen = jnp.where(s_idx <= q_blk, pen, jnp.where(s_idx < ns, NEG_INF, 0.0)).astype(BF16)
    pen_rows = jnp.concatenate([pen[kvh * nq:(kvh + 1) * nq, :] for kvh in range(N_KV_HEADS) for _ in range(GROUP)],
                               axis=0)
    q_aug = jnp.concatenate([qz, pen_rows], axis=1)

    ones_v = jnp.ones((KEY_CHUNK, LANES), BF16)

    def scores(c):
        k0 = pl.multiple_of(c * KEY_CHUNK, KEY_CHUNK)
        k_aug = jnp.concatenate([ksl_ref[pl.ds(k0, KEY_CHUNK), :], oh_ref[pl.ds(k0, KEY_CHUNK), :]], axis=1)
        return _dot_nt(q_aug, k_aug)

    def update(c, s, m_old, acc):
        k0 = pl.multiple_of(c * KEY_CHUNK, KEY_CHUNK)
        m_new = jnp.maximum(m_old, jnp.max(s, axis=-1, keepdims=True))
        p = jnp.exp2(s - m_new).astype(BF16)
        v_aug = jnp.concatenate([vsl_ref[pl.ds(k0, KEY_CHUNK), :], ones_v], axis=1)
        return m_new, jnp.exp2(m_old - m_new) * acc + _dot(p, v_aug)

    last = (i * nq) // KEY_CHUNK
    carry = (jnp.full((rows, 1), NEG_INF, F32), jnp.zeros((rows, 2 * LANES), F32))
    m_old, acc = lax.fori_loop(0, last, lambda c, cr: update(c, scores(c), *cr), carry)
    kpos = last * KEY_CHUNK + lax.broadcasted_iota(jnp.int32, (nq, KEY_CHUNK), 1)
    _, acc = update(last, _add_tile_bias(scores(last), jnp.where(kpos <= qpos_t, 0.0, NEG_INF)), m_old, acc)
    o_s = acc[:, :LANES] / acc[:, LANES:]

    out_ref[...] = _combine(gates_ref[...], ex_ref[...], _unstack_o(o_c, nq), _unstack_o(o_s, nq),
                            _unstack_o(o_w, nq), go_ref[...])


def _prompt_attn(q, kc, vc, kvb, gates, layer, w, consts):
    nb, s, _ = q.shape
    nc = kc.shape[1]
    ns = s // SEL_BLOCK
    col = lambda c: pl.BlockSpec((None, s, LANES), lambda b, i: (b, 0, c))
    whole = lambda a: pl.BlockSpec(a.shape, lambda b, i: (0,) * a.ndim)
    return pl.pallas_call(
        functools.partial(_prompt_attn_body, ns),
        grid=(nb, s // Q_BLOCK),
        in_specs=[pl.BlockSpec((None, Q_BLOCK, ATTN_DIM), lambda b, i: (b, i, 0)),
                  pl.BlockSpec((None, nc, LANES), lambda b, i: (b, 0, 0)),
                  pl.BlockSpec((None, nc, LANES), lambda b, i: (b, 0, 0)),
                  col(0), col(1), col(2), col(3),
                  whole(consts["onehot"]), whole(consts["ov"]),
                  pl.BlockSpec((None, Q_BLOCK, GATE_PAD), lambda b, i: (b, i, 0)),
                  whole(consts["expand"]),
                  pl.BlockSpec((None, 1, ATTN_DIM), lambda b, i: (layer, 0, 0))],
        out_specs=pl.BlockSpec((None, Q_BLOCK, ATTN_DIM), lambda b, i: (b, i, 0)),
        out_shape=jax.ShapeDtypeStruct((nb, s, ATTN_DIM), BF16),
        scratch_shapes=[pltpu.VMEM((LANES, N_KV_HEADS * Q_BLOCK), F32)],
        compiler_params=_params("arbitrary", "arbitrary"), name="prompt_attn",
    )(q, kc, vc, kvb, kvb, kvb, kvb, consts["onehot"], consts["ov"], gates, consts["expand"], w["g_attn_out"])


def _sample_attn_body(n_pages, nseq, pt_ref, q_ref, kvp_ref, kvw_ref, gates_ref, *rest):
    n_pg = nseq * n_pages
    pages = rest[:n_pg]
    (win_ref, wk_ref, wv_ref, bk_ref, bv_ref, w2k_ref, w2v_ref, perm_ref, oh_ref, ov_ref, ex_ref, go_ref,
     out_ref, wout_ref, sbuf) = rest[n_pg:]
    past = n_pages * PAGE_SIZE
    qpos = past
    nch = past // CMP_STRIDE
    per_page = PAGE_SIZE // CMP_STRIDE
    rows = N_HEADS
    lane = lax.broadcasted_iota(jnp.int32, (rows, LANES), 1)
    perm = perm_ref[...]

    def compress(r0, w1, b, w2):
        xs = [_dot_nt(perm, pg[r0:r0 + LANES, :].astype(BF16)) for pg in pages]
        acc = jnp.zeros((nseq * nch, 2 * LANES), F32)
        for j in range(CMP_STRIDE):
            lhs = jnp.concatenate([x[j * per_page:(j + 1) * per_page, :] for x in xs], axis=0)
            acc = acc + _dot(lhs.astype(BF16), w1[j])
        return [_cmp_finish(acc[sq * nch:(sq + 1) * nch, :], b[...], w2[...], sbuf).astype(BF16) for sq in range(nseq)]

    kcs = compress(0, wk_ref, bk_ref, w2k_ref)
    vcs = compress(LANES, wv_ref, bv_ref, w2v_ref)

    ns = (past + 1 + SEL_BLOCK - 1) // SEL_BLOCK
    q_blk = qpos // SEL_BLOCK
    sub = lax.broadcasted_iota(jnp.int32, (LANES, LANES), 0)
    lan = lax.broadcasted_iota(jnp.int32, (LANES, LANES), 1)
    w_len = win_ref.shape[2]
    eye_w = (lax.broadcasted_iota(jnp.int32, (2 * LANES, 2 * LANES), 0)
             == lax.broadcasted_iota(jnp.int32, (2 * LANES, 2 * LANES), 1))
    last_lane = lax.broadcasted_iota(jnp.int32, (2 * LANES, w_len), 1) == w_len - 1
    o_rows = ([], [], [])
    for sq in range(nseq):
        pgs = pages[sq * n_pages:(sq + 1) * n_pages]
        qz = _stack_q(q_ref[sq:sq + 1, :])
        qf = qz.astype(F32)
        kv_new = kvp_ref[sq:sq + 1, :]
        kvw_new = kvw_ref[sq:sq + 1, :]

        s_c = _dot_nt(qz, kcs[sq])
        blk = lax.broadcasted_iota(jnp.int32, (rows, nch), 1)
        p_c = _masked_softmax(s_c, (blk * CMP_STRIDE + (CMP_LEN - 1) <= qpos) & (blk < nch - 1))
        o_c = _dot(p_c.astype(BF16), vcs[sq])

        psum = jnp.concatenate([jnp.sum(p_c[kvh * GROUP:(kvh + 1) * GROUP, :], axis=0, keepdims=True)
                                for kvh in range(N_KV_HEADS)] + [jnp.zeros((rows - N_KV_HEADS, nch), F32)], axis=0)
        imp = _dot_exact_rhs(psum, ov_ref[...], 3)
        forced = (lane == 0) | (lane == q_blk) | (lane == q_blk - 1)
        imp = jnp.where(lane <= q_blk, jnp.where(forced, FORCE_SCORE, imp), -1.0)
        imp = jnp.where(lane < ns, imp, -2.0)
        imp_t = imp.T
        pens = []
        for kvh in range(N_KV_HEADS):
            other = imp_t[:, kvh:kvh + 1]
            mine = imp[kvh:kvh + 1, :]
            beats = (other > mine) | ((other == mine) & (sub < lan))
            cnt = jnp.sum(beats.astype(jnp.int32), axis=0, keepdims=True)
            sel = (cnt < N_SEL) & (lane[0:1, :] <= q_blk)
            pens.append(jnp.where(sel | (lane[0:1, :] >= ns), 0.0, NEG_INF))
        pen = jnp.concatenate([pens[kvh] for kvh in range(N_KV_HEADS) for _ in range(GROUP)], axis=0)
        q_aug = jnp.concatenate([qz, pen.astype(BF16)], axis=1)

        s_s = jnp.concatenate(
            [_dot(q_aug, jnp.concatenate([pg[2 * LANES:3 * LANES, :].astype(BF16),
                                          oh_ref[:, p * PAGE_SIZE:(p + 1) * PAGE_SIZE]], axis=0))
             for p, pg in enumerate(pgs)], axis=1)
        k_new = kv_new[:, 2 * LANES:3 * LANES].astype(BF16).astype(F32)
        v_new = kv_new[:, 3 * LANES:4 * LANES].astype(BF16).astype(F32)
        pen_new = jnp.sum(jnp.where(lane == q_blk, pen, 0.0), axis=-1, keepdims=True)
        s_new = jnp.sum(qf * k_new, axis=-1, keepdims=True) + pen_new
        m = jnp.maximum(jnp.max(s_s, axis=-1, keepdims=True), s_new)
        p_s = jnp.exp2(s_s - m)
        p_new = jnp.exp2(s_new - m)
        den = jnp.sum(p_s, axis=-1, keepdims=True) + p_new
        p_sb = p_s.astype(BF16)
        o_s = p_new.astype(BF16).astype(F32) * v_new
        for p, pg in enumerate(pgs):
            o_s = o_s + _dot_nt(p_sb[:, p * PAGE_SIZE:(p + 1) * PAGE_SIZE], pg[3 * LANES:4 * LANES, :].astype(BF16))
        o_s = o_s / den

        win_t = win_ref[sq]
        s_w = _dot(qz, win_t[:LANES, :].astype(BF16))
        diff = qpos - (past - w_len + lax.broadcasted_iota(jnp.int32, (rows, w_len), 1))
        ok = (diff >= 0) & (diff < WINDOW)
        s_wn = jnp.sum(qf * kvw_new[:, :LANES].astype(BF16).astype(F32), axis=-1, keepdims=True)
        m = jnp.maximum(jnp.max(jnp.where(ok, s_w, NEG_INF), axis=-1, keepdims=True), s_wn)
        p_w = jnp.where(ok, jnp.exp2(s_w - m), 0.0)
        p_wn = jnp.exp2(s_wn - m)
        den = jnp.sum(p_w, axis=-1, keepdims=True) + p_wn
        o_w = (_dot_nt(p_w.astype(BF16), win_t[LANES:, :].astype(BF16))
               + p_wn.astype(BF16).astype(F32) * kvw_new[:, LANES:].astype(BF16).astype(F32)) / den

        for dst, o in zip(o_rows, (o_c, o_s, o_w)):
            dst.append(_unstack_o(o, 1))
        new_col = jnp.sum(jnp.where(eye_w, kvw_new, 0.0), axis=1, keepdims=True)
        wout_ref[sq] = jnp.where(last_lane, new_col, pltpu.roll(win_t, w_len - 1, 1))

    o_c, o_s, o_w = (jnp.concatenate(r, axis=0) for r in o_rows)
    out_ref[...] = _combine(gates_ref[...], ex_ref[...], o_c, o_s, o_w, go_ref[...])


def _sample_attn(q, kvp, kvw, gates, cache_kv_t, cache_win_t, page_table, layer, w, consts, nseq=4):
    db, n_pages = page_table.shape
    w_len = cache_win_t.shape[3]
    nch = n_pages * PAGE_SIZE // CMP_STRIDE
    assert db % nseq == 0
    r3 = lambda a: a.reshape(db // nseq, nseq, a.shape[-1])
    rowspec = lambda c: pl.BlockSpec((None, nseq, c), lambda b, pt: (b, 0, 0))
    whole = lambda a: pl.BlockSpec(a.shape, lambda b, pt: (0,) * a.ndim)
    wspec = lambda a: pl.BlockSpec((None,) + a.shape[1:], lambda b, pt: (layer,) + (0,) * (a.ndim - 1))
    page_specs = [pl.BlockSpec((None, None, N_PAGED * LANES, PAGE_SIZE),
                               functools.partial(lambda sq, p, b, pt: (layer, pt[b * nseq + sq, p], 0, 0), sq, p))
                  for sq in range(nseq) for p in range(n_pages)]
    ws = [w["cmp_w1_k"], w["cmp_w1_v"], w["cmp_b_k"], w["cmp_b_v"], w["cmp_w2_k"], w["cmp_w2_v"]]
    cs = [consts["perm"], consts["onehot_s"], consts["ov_s"], consts["expand"]]
    grid_spec = pltpu.PrefetchScalarGridSpec(
        num_scalar_prefetch=1, grid=(db // nseq,),
        in_specs=[rowspec(ATTN_DIM), rowspec(N_PAGED * LANES), rowspec(2 * LANES), rowspec(GATE_PAD)] + page_specs
        + [pl.BlockSpec((None, nseq, 2 * LANES, w_len), lambda b, pt: (layer, b, 0, 0))]
        + [wspec(a) for a in ws] + [whole(a) for a in cs] + [wspec(w["g_attn_out"])],
        out_specs=[rowspec(ATTN_DIM), pl.BlockSpec((nseq, 2 * LANES, w_len), lambda b, pt: (b, 0, 0))],
        scratch_shapes=[pltpu.VMEM((nch + 8, LANES), F32)],
    )
    out, wout = pl.pallas_call(
        functools.partial(_sample_attn_body, n_pages, nseq),
        grid_spec=grid_spec,
        out_shape=[jax.ShapeDtypeStruct((db // nseq, nseq, ATTN_DIM), BF16),
                   jax.ShapeDtypeStruct((db, 2 * LANES, w_len), F32)],
        compiler_params=_params("arbitrary"), name="sample_attn",
    )(page_table, r3(q), r3(kvp), r3(kvw), r3(gates), *([cache_kv_t] * (nseq * n_pages)), cache_win_t, *ws, *cs,
      w["g_attn_out"])
    return out.reshape(db, ATTN_DIM), wout


def _postmix_body(final, x_ref, yc_ref, ya_ref, woc_ref, woa_ref, gf_ref, wg_ref, wu_ref, wd_ref, gl_ref, xo_ref, *rest):
    h_s, acc_s = rest[-2:]
    f = pl.program_id(1)

    @pl.when(f == 0)
    def _():
        x1 = x_ref[...] + _dot(yc_ref[...], woc_ref[...]) + _dot(ya_ref[...], woa_ref[...])
        h_s[...] = _rms(x1, gf_ref[...]).astype(BF16)
        acc_s[...] = x1

    h = h_s[...]
    a = _dot(h, wg_ref[...])
    act = (a * jax.nn.sigmoid(a) * _dot(h, wu_ref[...])).astype(BF16)
    acc_s[...] += _dot(act, wd_ref[...])

    @pl.when(f == pl.num_programs(1) - 1)
    def _():
        out = acc_s[...]
        xo_ref[...] = out
        if final:
            rest[0][...] = _rms(out, gl_ref[...])


def _postmix(x, ycn, yan, layer, w, final, tm=512, n_ff=2):
    t, d = x.shape
    tm = min(tm, t)
    d_ff = w["w_gate"].shape[-1]
    tf = d_ff // n_ff
    assert tf * n_ff == d_ff and tf % LANES == 0
    row = lambda c: pl.BlockSpec((tm, c), lambda i, f: (i, 0))
    wspec = lambda a: pl.BlockSpec((None,) + a.shape[1:], lambda i, f: (layer,) + (0,) * (a.ndim - 1))
    gl = w["g_final"]
    n_out = 2 if final else 1
    outs = pl.pallas_call(
        functools.partial(_postmix_body, final),
        grid=(t // tm, n_ff),
        in_specs=[row(d), row(CONV_DIM), row(ATTN_DIM), wspec(w["w_out_c"]), wspec(w["w_out_a"]), wspec(w["g_ffn"]),
                  pl.BlockSpec((None, d, tf), lambda i, f: (layer, 0, f)),
                  pl.BlockSpec((None, d, tf), lambda i, f: (layer, 0, f)),
                  pl.BlockSpec((None, tf, d), lambda i, f: (layer, f, 0)),
                  pl.BlockSpec(gl.shape, lambda i, f: (0, 0))],
        out_specs=[row(d)] * n_out,
        out_shape=[jax.ShapeDtypeStruct((t, d), F32)] * n_out,
        scratch_shapes=[pltpu.VMEM((tm, d), BF16), pltpu.VMEM((tm, d), F32)],
        compiler_params=_params("arbitrary", "arbitrary"), name="postmix",
    )(x, ycn, yan, w["w_out_c"], w["w_out_a"], w["g_ffn"], w["w_gate"], w["w_up"], w["w_down"], gl)
    return outs


def _prep_weights(g_mix, w_in, conv_w, cmp_pe_k, cmp_w1_k, cmp_w2_k, cmp_pe_v, cmp_w1_v, cmp_w2_v, g_conv_out,
                  g_attn_out, w_out, g_ffn, w_gate, w_up, w_down, g_final):
    nl, d, _ = w_in.shape
    c3 = 3 * CONV_DIM
    kv_cols = 6 * LANES

    def perm_heads(a, axis):
        shp = a.shape
        a = a.reshape(shp[:axis] + (N_KV_HEADS, GROUP, HEAD_DIM) + shp[axis + 1:])
        a = jnp.swapaxes(a, axis, axis + 1)
        return a.reshape(shp)

    def blockdiag2(a):
        z = jnp.zeros_like(a)
        return jnp.concatenate([jnp.concatenate([a, z], axis=-1), jnp.concatenate([z, a], axis=-1)], axis=-2)

    def cmp_w1(w1):
        a = blockdiag2(w1[:, :CMP_STRIDE])
        b = blockdiag2(w1[:, CMP_STRIDE:])
        return jnp.concatenate([a, b], axis=-1).astype(BF16)

    pe = jnp.stack([cmp_pe_k, cmp_pe_v], axis=1).reshape(nl, 2, CMP_LEN * HEAD_DIM, 1)
    w1 = jnp.stack([cmp_w1_k, cmp_w1_v], axis=1).reshape(nl, 2, CMP_LEN * HEAD_DIM, HEAD_DIM)
    bias = _cmp_bias(pe, w1)
    bias2 = jnp.concatenate([bias, bias], axis=-1)
    gates_w = jnp.pad(w_in[:, :, c3 + ATTN_DIM + kv_cols:], ((0, 0), (0, 0), (0, GATE_PAD - 3 * N_HEADS)))
    return {
        "g_mix": g_mix.reshape(nl, 1, d),
        "w_c": w_in[:, :, :c3].astype(BF16),
        "w_q": perm_heads(w_in[:, :, c3:c3 + ATTN_DIM], 2).astype(BF16),
        "w_kv": w_in[:, :, c3 + ATTN_DIM:c3 + ATTN_DIM + kv_cols].astype(BF16),
        "w_g": gates_w.astype(BF16),
        "conv_w": conv_w,
        "g_conv_out": g_conv_out.reshape(nl, 1, CONV_DIM),
        "cmp_w1_k": cmp_w1(cmp_w1_k), "cmp_w1_v": cmp_w1(cmp_w1_v),
        "cmp_b_k": bias2[:, 0], "cmp_b_v": bias2[:, 1],
        "cmp_w2_k": blockdiag2(cmp_w2_k).astype(BF16), "cmp_w2_v": blockdiag2(cmp_w2_v).astype(BF16),
        "g_attn_out": perm_heads(g_attn_out, 1).reshape(nl, 1, ATTN_DIM),
        "w_out_c": w_out[:, :CONV_DIM].astype(BF16),
        "w_out_a": perm_heads(w_out[:, CONV_DIM:], 1).astype(BF16),
        "g_ffn": g_ffn.reshape(nl, 1, d),
        "w_gate": w_gate.astype(BF16), "w_up": w_up.astype(BF16), "w_down": w_down.astype(BF16),
        "g_final": g_final.reshape(1, d),
    }


def kernel(x_prompt, x_sample, cache_kv, cache_win, state_conv, page_table, g_mix, w_in, conv_w, cmp_pe_k, cmp_w1_k, cmp_w2_k, cmp_pe_v, cmp_w1_v, cmp_w2_v, g_conv_out, g_attn_out, w_out, g_ffn, w_gate, w_up, w_down, g_final):
    nl = w_in.shape[0]
    b, s, d = x_prompt.shape
    db, sq = x_sample.shape[:2]
    assert sq == 1 and s % 512 == 0
    n_pages = page_table.shape[1]
    past = n_pages * PAGE_SIZE
    w_len = cache_win.shape[2]
    w_keep = min(WINDOW, s)
    w = _prep_weights(g_mix, w_in, conv_w, cmp_pe_k, cmp_w1_k, cmp_w2_k, cmp_pe_v, cmp_w1_v, cmp_w2_v, g_conv_out,
                      g_attn_out, w_out, g_ffn, w_gate, w_up, w_down, g_final)
    consts = {
        "expand": _gate_expand(),
        "onehot": _sel_onehot(s), "ov": _cmp_to_sel(s // CMP_STRIDE, s // SEL_BLOCK),
        "onehot_s": _sel_onehot(past).T, "ov_s": _cmp_to_sel(past // CMP_STRIDE, (past + 1 + SEL_BLOCK - 1) // SEL_BLOCK),
        "perm": _chunk_perm(),
    }
    cache_kv = cache_kv.transpose(0, 1, 3, 4, 5, 2).reshape(nl, cache_kv.shape[1], N_PAGED * LANES, PAGE_SIZE)
    cache_win = cache_win.transpose(0, 1, 3, 4, 5, 2).reshape(nl, db, 2 * LANES, w_len)

    xp, xs = x_prompt, x_sample
    kvp_l, winp_l, convp_l, kvs_l, wins_l, convs_l = [], [], [], [], [], []
    for l in range(nl):
        final = l == nl - 1
        ycn, q, kvp, kvw, kvb, gates, tail = _premix(xp, l, w)
        kc, vc = _compress(kvp, l, w)
        yan = _prompt_attn(q, kc, vc, kvb, gates, l, w, consts)
        outs = _postmix(xp.reshape(b * s, d), ycn.reshape(b * s, -1), yan.reshape(b * s, -1), l, w, final)
        xp = outs[0].reshape(b, s, d)
        if final:
            y_prompt = outs[1].reshape(b, s, d)
        kvp_l.append(kvp)
        winp_l.append(kvw[:, s - w_keep:])
        convp_l.append(tail[:, 8 - (CONV_WIDTH - 1):])
        ycn, q, kvp, kvw, kvb, gates, u = _premix(xs, l, w, prev=(state_conv[l, :, 1], state_conv[l, :, 0]))
        yan, wout = _sample_attn(q, kvp, kvw, gates, cache_kv, cache_win, page_table, l, w, consts)
        outs = _postmix(xs.reshape(db, d), ycn, yan, l, w, final)
        xs = outs[0].reshape(db, 1, d)
        if final:
            y_sample = outs[1].reshape(db, 1, d)
        kvs_l.append(kvp)
        wins_l.append(wout)
        convs_l.append(jnp.stack([state_conv[l, :, 1], u], axis=1))
    kv_shape = (N_PAGED, N_KV_HEADS, HEAD_DIM)
    return (y_prompt, y_sample,
            jnp.stack(kvp_l).reshape((nl, b, s) + kv_shape),
            jnp.stack(winp_l).reshape((nl, b, w_keep, 2) + kv_shape[1:]),
            jnp.stack(convp_l),
            jnp.stack(kvs_l).reshape((nl, db, 1) + kv_shape),
            jnp.stack(wins_l).reshape((nl, db, 2) + kv_shape[1:] + (w_len,)).transpose(0, 1, 5, 2, 3, 4),
            jnp.stack(convs_l))
```

```python
import functools

import numpy as np
import jax
import jax.numpy as jnp
from jax import lax
from jax.experimental import pallas as pl
from jax.experimental.pallas import tpu as pltpu

HEAD_DIM = 64
N_KV_HEADS = 2
GROUP = 4
N_HEADS = N_KV_HEADS * GROUP
LANES = 2 * HEAD_DIM
CONV_DIM = 512
ATTN_DIM = N_HEADS * HEAD_DIM
CONV_WIDTH = 3
CMP_STRIDE = 16
CMP_LEN = 2 * CMP_STRIDE
SEL_BLOCK = 64
N_SEL = 16
WINDOW = 512
Q_BLOCK = 128
PAGE_SIZE = 128
N_PAGED = 4
RMS_EPS = 1e-6
NEG_INF = -1e30
FORCE_SCORE = 1e4
SCALE = HEAD_DIM ** -0.5
Q_SCALE = SCALE * 1.4426950408889634
GATE_PAD = 128
KEY_CHUNK = 512
VMEM_LIMIT = 56 * 1024 * 1024

F32 = jnp.float32
BF16 = jnp.bfloat16


def _rms(x, g):
    return x * lax.rsqrt(jnp.mean(x * x, axis=-1, keepdims=True) + RMS_EPS) * g


def _dot(a, b):
    return jnp.dot(a, b, preferred_element_type=F32)


def _dot_nt(a, b):
    return lax.dot_general(a, b, (((1,), (1,)), ((), ())), preferred_element_type=F32)


def _halves(dot, a, b):
    h = a.shape[0] // 2
    return jnp.concatenate([dot(a[:h], b), dot(a[h:], b)], axis=0)


def _params(*sem):
    return pltpu.CompilerParams(dimension_semantics=sem, vmem_limit_bytes=VMEM_LIMIT)


def _premix_body(seq_mode, tm, x_ref, g_ref, wc_ref, wq_ref, wkv_ref, wg_ref, cw_ref, gco_ref, *rest):
    if seq_mode:
        ycn_ref, q_ref, kvp_ref, kvw_ref, kvb_ref, gates_ref, tail_ref, ubuf = rest
    else:
        p1_ref, p2_ref, ycn_ref, q_ref, kvp_ref, kvw_ref, kvb_ref, gates_ref, tail_ref = rest
    x = x_ref[...]
    xn = _rms(x, g_ref[...]).astype(BF16)
    pc = _dot(xn, wc_ref[...])
    bg, u = pc[:, :CONV_DIM], pc[:, CONV_DIM:2 * CONV_DIM] * pc[:, 2 * CONV_DIM:]
    if seq_mode:
        @pl.when(pl.program_id(1) == 0)
        def _():
            ubuf[0:8, :] = jnp.zeros((8, CONV_DIM), F32)
        ubuf[8:8 + tm, :] = u
        up1 = ubuf[7:7 + tm, :]
        up2 = ubuf[6:6 + tm, :]
        ubuf[0:8, :] = ubuf[tm:tm + 8, :]
        tail_ref[...] = u[tm - 8:, :]
    else:
        up1, up2 = p1_ref[...], p2_ref[...]
        tail_ref[...] = u
    cw = cw_ref[...]
    yc = bg * (cw[0:1, :] * up2 + cw[1:2, :] * up1 + cw[2:3, :] * u)
    ycn_ref[...] = _rms(yc, gco_ref[...]).astype(BF16)
    q_ref[...] = (_dot(xn, wq_ref[...]) * Q_SCALE).astype(BF16)
    kv = _dot(xn, wkv_ref[...])
    kvp_ref[...] = kv[:, :N_PAGED * LANES]
    kvw_ref[...] = kv[:, N_PAGED * LANES:]
    kvb_ref[...] = kv[:, 2 * LANES:].astype(BF16)
    gates_ref[...] = jax.nn.sigmoid(_dot(xn, wg_ref[...]))


def _premix(x, layer, w, prev=None, tm=512):
    nb, s, d = x.shape
    seq_mode = prev is None
    if seq_mode:
        grid = (nb, s // tm)
        row = lambda c: pl.BlockSpec((None, tm, c), lambda b, i: (b, i, 0))
        tail_spec = pl.BlockSpec((None, 8, CONV_DIM), lambda b, i: (b, 0, 0))
        tail_shape = (nb, 8, CONV_DIM)
        xin = x
        lead = (nb, s)
    else:
        tm = nb
        grid = (1, 1)
        row = lambda c: pl.BlockSpec((tm, c), lambda b, i: (0, 0))
        tail_spec = row(CONV_DIM)
        tail_shape = (nb, CONV_DIM)
        xin = x.reshape(nb, d)
        lead = (nb,)
    wspec = lambda a: pl.BlockSpec((None,) + a.shape[1:], lambda b, i: (layer,) + (0,) * (a.ndim - 1))
    ws = [w["g_mix"], w["w_c"], w["w_q"], w["w_kv"], w["w_g"], w["conv_w"], w["g_conv_out"]]
    in_specs = [row(d)] + [wspec(a) for a in ws]
    args = [xin] + ws
    scratch = []
    if seq_mode:
        scratch = [pltpu.VMEM((tm + 8, CONV_DIM), F32)]
    else:
        in_specs += [row(CONV_DIM), row(CONV_DIM)]
        args += list(prev)
    out_shape = [
        jax.ShapeDtypeStruct(lead + (CONV_DIM,), BF16),
        jax.ShapeDtypeStruct(lead + (ATTN_DIM,), BF16),
        jax.ShapeDtypeStruct(lead + (N_PAGED * LANES,), F32),
        jax.ShapeDtypeStruct(lead + (2 * LANES,), F32),
        jax.ShapeDtypeStruct(lead + (4 * LANES,), BF16),
        jax.ShapeDtypeStruct(lead + (GATE_PAD,), F32),
        jax.ShapeDtypeStruct(tail_shape, F32),
    ]
    out_specs = [row(CONV_DIM), row(ATTN_DIM), row(N_PAGED * LANES), row(2 * LANES), row(4 * LANES),
                 row(GATE_PAD), tail_spec]
    return pl.pallas_call(
        functools.partial(_premix_body, seq_mode, tm),
        grid=grid, in_specs=in_specs, out_specs=out_specs, out_shape=out_shape,
        scratch_shapes=scratch, compiler_params=_params("arbitrary", "arbitrary"),
        name="premix_seq" if seq_mode else "premix_step",
    )(*args)


def _cmp_bias_body(pe_ref, w1_ref, o_ref):
    o_ref[...] = jnp.sum(pe_ref[...] * w1_ref[...], axis=0, keepdims=True)


def _cmp_bias(pe, w1):
    nl = pe.shape[0]
    k = CMP_LEN * HEAD_DIM
    return pl.pallas_call(
        _cmp_bias_body,
        grid=(nl, 2),
        in_specs=[pl.BlockSpec((None, None, k, 1), lambda l, t: (l, t, 0, 0)),
                  pl.BlockSpec((None, None, k, HEAD_DIM), lambda l, t: (l, t, 0, 0))],
        out_specs=pl.BlockSpec((None, None, 1, HEAD_DIM), lambda l, t: (l, t, 0, 0)),
        out_shape=jax.ShapeDtypeStruct((nl, 2, 1, HEAD_DIM), F32),
        compiler_params=_params("arbitrary", "arbitrary"), name="cmp_bias",
    )(pe, w1)


def _cmp_finish(acc, bias, w2, sbuf):
    n = acc.shape[0]
    sbuf[0:n, :] = acc[:, LANES:]
    sbuf[n:n + 8, :] = jnp.zeros((8, LANES), F32)
    h = jax.nn.gelu(acc[:, :LANES] + sbuf[1:n + 1, :] + bias)
    return _dot(h.astype(BF16), w2)


def _compress_body(nch, k_ref, v_ref, wk_ref, wv_ref, bk_ref, bv_ref, w2k_ref, w2v_ref, kc_ref, vc_ref, sbuf):
    for src, w1, b, w2, out in ((k_ref, wk_ref, bk_ref, w2k_ref, kc_ref), (v_ref, wv_ref, bv_ref, w2v_ref, vc_ref)):
        acc = jnp.zeros((nch, 2 * LANES), F32)
        for t in range(CMP_STRIDE // 2):
            lhs = jnp.concatenate([src[pl.ds(j, nch, stride=CMP_STRIDE), :] for j in (2 * t, 2 * t + 1)], axis=1)
            acc = acc + _dot(lhs.astype(BF16), w1[t])
        out[...] = _cmp_finish(acc, b[...], w2[...], sbuf).astype(BF16)


def _compress(kvp, layer, w):
    nb, s, _ = kvp.shape
    nch = s // CMP_STRIDE
    wspec = lambda a: pl.BlockSpec((None,) + a.shape[1:], lambda b: (layer,) + (0,) * (a.ndim - 1))
    ws = [w["cmp_w1_k"], w["cmp_w1_v"], w["cmp_b_k"], w["cmp_b_v"], w["cmp_w2_k"], w["cmp_w2_v"]]
    return pl.pallas_call(
        functools.partial(_compress_body, nch),
        grid=(nb,),
        in_specs=[pl.BlockSpec((None, s, LANES), lambda b: (b, 0, 0)),
                  pl.BlockSpec((None, s, LANES), lambda b: (b, 0, 1))] + [wspec(a) for a in ws],
        out_specs=[pl.BlockSpec((None, nch, LANES), lambda b: (b, 0, 0))] * 2,
        out_shape=[jax.ShapeDtypeStruct((nb, nch, LANES), BF16)] * 2,
        scratch_shapes=[pltpu.VMEM((nch + 8, LANES), F32)],
        compiler_params=_params("arbitrary"), name="compress",
    )(kvp, kvp, *ws)


def _half_mask(rows):
    return lax.broadcasted_iota(jnp.int32, (rows, LANES), 1) < HEAD_DIM


def _stack_q(qt):
    n = qt.shape[0]
    lo = _half_mask(n)
    zero = jnp.zeros((n, LANES), qt.dtype)
    parts = []
    for kvh in range(N_KV_HEADS):
        for g in range(GROUP):
            blk = qt[:, g * LANES:(g + 1) * LANES]
            parts.append(jnp.where(lo if kvh == 0 else ~lo, blk, zero))
    return jnp.concatenate(parts, axis=0)


def _unstack_o(o, n):
    lo = _half_mask(n)
    cols = [jnp.where(lo, o[g * n:(g + 1) * n, :], o[(GROUP + g) * n:(GROUP + g + 1) * n, :]) for g in range(GROUP)]
    return jnp.concatenate(cols, axis=1)


def _masked_softmax(s, mask):
    m = jnp.max(jnp.where(mask, s, NEG_INF), axis=-1, keepdims=True)
    e = jnp.where(mask, jnp.exp2(s - m), 0.0)
    l = jnp.sum(e, axis=-1, keepdims=True)
    return e / jnp.where(l > 0.0, l, 1.0)


def _add_tile_bias(s, bias):
    n = bias.shape[0]
    return jnp.concatenate([s[r * n:(r + 1) * n, :] + bias for r in range(s.shape[0] // n)], axis=0)


def _split_bf16(x, terms):
    out = []
    for _ in range(terms):
        hi = x.astype(BF16)
        out.append(hi)
        x = x - hi.astype(F32)
    return out


def _dot_exact_rhs(x, rhs_bf16, terms):
    return sum(_dot(t, rhs_bf16) for t in _split_bf16(x, terms))


def _combine(gates, expand, o_c, o_s, o_w, g_out):
    gx = _dot_exact_rhs(gates, expand, 2)
    ya = gx[:, :ATTN_DIM] * o_c + gx[:, ATTN_DIM:2 * ATTN_DIM] * o_s + gx[:, 2 * ATTN_DIM:] * o_w
    return _rms(ya, g_out).astype(BF16)


def _cmp_to_sel(nc_rows, ns):
    n = np.arange(nc_rows)[:, None]
    s = np.arange(LANES)[None, :]
    ov = (n * CMP_STRIDE <= s * SEL_BLOCK + SEL_BLOCK - 1) & (n * CMP_STRIDE + CMP_LEN - 1 >= s * SEL_BLOCK) & (s < ns)
    return jnp.asarray(ov.astype(np.float32), dtype=BF16)


def _gate_expand():
    e = np.zeros((GATE_PAD, 3 * ATTN_DIM), np.float32)
    for br in range(3):
        for kvh in range(N_KV_HEADS):
            for g in range(GROUP):
                c0 = br * ATTN_DIM + g * LANES + kvh * HEAD_DIM
                e[br * N_HEADS + kvh * GROUP + g, c0:c0 + HEAD_DIM] = 1.0
    return jnp.asarray(e, dtype=BF16)


def _sel_onehot(n_keys):
    t = np.arange(n_keys)[:, None] // SEL_BLOCK
    return jnp.asarray((t == np.arange(LANES)[None, :]).astype(np.float32), dtype=BF16)


def _chunk_perm():
    m = np.arange(PAGE_SIZE)[:, None]
    t = np.arange(PAGE_SIZE)[None, :]
    per_page = PAGE_SIZE // CMP_STRIDE
    return jnp.asarray((t == CMP_STRIDE * (m % per_page) + m // per_page).astype(np.float32), dtype=BF16)


def _rank_penalty(imp_t, ns):
    r = imp_t.shape[1]
    sub = lax.broadcasted_iota(jnp.int32, (8, r), 0)
    cnt = jnp.zeros((ns, r), F32)
    for sp in range(ns):
        other = imp_t[sp:sp + 1, :]
        parts = []
        for blk in range(ns // 8):
            t = imp_t[blk * 8:(blk + 1) * 8, :]
            ge = jnp.where(other >= t, 1.0, 0.0)
            gt = jnp.where(other > t, 1.0, 0.0)
            if blk * 8 > sp:
                parts.append(ge)
            elif blk * 8 + 7 <= sp:
                parts.append(gt)
            else:
                parts.append(jnp.where(sub > sp - blk * 8, ge, gt))
        cnt = cnt + jnp.concatenate(parts, axis=0)
    return jnp.where(cnt < N_SEL, 0.0, NEG_INF)


def _prompt_attn_body(ns, q_ref, kc_ref, vc_ref, ksl_ref, vsl_ref, kw_ref, vw_ref, oh_ref, ov_ref, gates_ref,
                      ex_ref, go_ref, out_ref, pen_s):
    i = pl.program_id(1)
    nq = Q_BLOCK
    rows = N_HEADS * nq
    qz = _stack_q(q_ref[...])
    qpos_t = i * nq + lax.broadcasted_iota(jnp.int32, (nq, 1), 0)

    kc = kc_ref[...]
    nc = kc.shape[0]
    c_end = lax.broadcasted_iota(jnp.int32, (nq, nc), 1) * CMP_STRIDE + (CMP_LEN - 1)
    s_c = _add_tile_bias(_halves(_dot_nt, qz, kc), jnp.where(c_end <= qpos_t, 0.0, NEG_INF))
    e_c = jnp.exp2(s_c - jnp.max(s_c, axis=-1, keepdims=True))
    any_c = jnp.where(qpos_t >= CMP_LEN - 1, 1.0, 0.0)
    p_c = e_c * (jnp.concatenate([any_c] * N_HEADS, axis=0) / jnp.sum(e_c, axis=-1, keepdims=True))
    o_c = _dot(p_c.astype(BF16), vc_ref[...])

    psum = jnp.concatenate(
        [sum(p_c[(kvh * GROUP + g) * nq:(kvh * GROUP + g + 1) * nq, :] for g in range(GROUP))
         for kvh in range(N_KV_HEADS)], axis=0)
    imp = _dot_exact_rhs(psum, ov_ref[...], 3)
    r2 = N_KV_HEADS * nq
    s_idx = lax.broadcasted_iota(jnp.int32, (r2, LANES), 1)
    q_blk = (i * nq + lax.broadcasted_iota(jnp.int32, (r2, LANES), 0) % nq) // SEL_BLOCK
    forced = (s_idx == 0) | (s_idx == q_blk) | (s_idx == q_blk - 1)
    imp = jnp.where(s_idx <= q_blk, jnp.where(forced, FORCE_SCORE, imp), -1.0)
    imp = jnp.where(s_idx < ns, imp, -2.0)
    pen_s[...] = jnp.zeros((LANES, r2), F32)

    nw = WINDOW + nq
    w0 = pl.multiple_of(jnp.maximum(i * nq - WINDOW, 0), nq)
    diff = qpos_t - (w0 + lax.broadcasted_iota(jnp.int32, (nq, nw), 1))
    s_w = _add_tile_bias(_halves(_dot_nt, qz, kw_ref[pl.ds(w0, nw), :]),
                         jnp.where((diff >= 0) & (diff < WINDOW), 0.0, NEG_INF))
    e_w = jnp.exp2(s_w - jnp.max(s_w, axis=-1, keepdims=True))
    o_w = _dot(e_w.astype(BF16), vw_ref[pl.ds(w0, nw), :]) / jnp.sum(e_w, axis=-1, keepdims=True)

    @pl.when((i + 1) * nq > N_SEL * SEL_BLOCK)
    def _():
        pen_s[0:ns, :] = _rank_penalty(imp.T[0:ns, :], ns)

    pen = pen_s[...].T
    pen = jnp.where(s_idx <= q_blk, pen, jnp.where(s_idx < ns, NEG_INF, 0.0)).astype(BF16)
    pen_rows = jnp.concatenate([pen[kvh * nq:(kvh + 1) * nq, :] for kvh in range(N_KV_HEADS) for _ in range(GROUP)],
                               axis=0)
    q_aug = jnp.concatenate([qz, pen_rows], axis=1)

    ones_v = jnp.ones((KEY_CHUNK, LANES), BF16)

    def scores(c):
        k0 = pl.multiple_of(c * KEY_CHUNK, KEY_CHUNK)
        k_aug = jnp.concatenate([ksl_ref[pl.ds(k0, KEY_CHUNK), :], oh_ref[pl.ds(k0, KEY_CHUNK), :]], axis=1)
        return _halves(_dot_nt, q_aug, k_aug)

    def update(c, s, m_old, acc):
        k0 = pl.multiple_of(c * KEY_CHUNK, KEY_CHUNK)
        m_new = jnp.maximum(m_old, jnp.max(s, axis=-1, keepdims=True))
        p = jnp.exp2(s - m_new).astype(BF16)
        v_aug = jnp.concatenate([vsl_ref[pl.ds(k0, KEY_CHUNK), :], ones_v], axis=1)
        return m_new, jnp.exp2(m_old - m_new) * acc + _dot(p, v_aug)

    last = (i * nq) // KEY_CHUNK
    carry = (jnp.full((rows, 1), NEG_INF, F32), jnp.zeros((rows, 2 * LANES), F32))
    m_old, acc = lax.fori_loop(0, last, lambda c, cr: update(c, scores(c), *cr), carry)
    kpos = last * KEY_CHUNK + lax.broadcasted_iota(jnp.int32, (nq, KEY_CHUNK), 1)
    _, acc = update(last, _add_tile_bias(scores(last), jnp.where(kpos <= qpos_t, 0.0, NEG_INF)), m_old, acc)
    o_s = acc[:, :LANES] / acc[:, LANES:]

    out_ref[...] = _combine(gates_ref[...], ex_ref[...], _unstack_o(o_c, nq), _unstack_o(o_s, nq),
                            _unstack_o(o_w, nq), go_ref[...])


def _prompt_attn(q, kc, vc, kvb, gates, layer, w, consts):
    nb, s, _ = q.shape
    nc = kc.shape[1]
    ns = s // SEL_BLOCK
    col = lambda c: pl.BlockSpec((None, s, LANES), lambda b, i: (b, 0, c))
    whole = lambda a: pl.BlockSpec(a.shape, lambda b, i: (0,) * a.ndim)
    return pl.pallas_call(
        functools.partial(_prompt_attn_body, ns),
        grid=(nb, s // Q_BLOCK),
        in_specs=[pl.BlockSpec((None, Q_BLOCK, ATTN_DIM), lambda b, i: (b, i, 0)),
                  pl.BlockSpec((None, nc, LANES), lambda b, i: (b, 0, 0)),
                  pl.BlockSpec((None, nc, LANES), lambda b, i: (b, 0, 0)),
                  col(0), col(1), col(2), col(3),
                  whole(consts["onehot"]), whole(consts["ov"]),
                  pl.BlockSpec((None, Q_BLOCK, GATE_PAD), lambda b, i: (b, i, 0)),
                  whole(consts["expand"]),
                  pl.BlockSpec((None, 1, ATTN_DIM), lambda b, i: (layer, 0, 0))],
        out_specs=pl.BlockSpec((None, Q_BLOCK, ATTN_DIM), lambda b, i: (b, i, 0)),
        out_shape=jax.ShapeDtypeStruct((nb, s, ATTN_DIM), BF16),
        scratch_shapes=[pltpu.VMEM((LANES, N_KV_HEADS * Q_BLOCK), F32)],
        compiler_params=_params("arbitrary", "arbitrary"), name="prompt_attn",
    )(q, kc, vc, kvb, kvb, kvb, kvb, consts["onehot"], consts["ov"], gates, consts["expand"], w["g_attn_out"])


def _sample_attn_body(n_pages, nseq, pt_ref, q_ref, kvp_ref, kvw_ref, gates_ref, *rest):
    n_pg = nseq * n_pages
    pages = rest[:n_pg]
    (win_ref, wk_ref, wv_ref, bk_ref, bv_ref, w2k_ref, w2v_ref, perm_ref, oh_ref, ov_ref, ex_ref, go_ref,
     out_ref, wout_ref, sbuf) = rest[n_pg:]
    past = n_pages * PAGE_SIZE
    qpos = past
    nch = past // CMP_STRIDE
    per_page = PAGE_SIZE // CMP_STRIDE
    seqs = range(nseq)
    R = nseq * N_HEADS
    perm = perm_ref[...]

    xs = [_dot_nt(perm, pg[0:2 * LANES, :].astype(BF16)) for pg in pages]

    def compress(c0, w1, b, w2):
        acc = jnp.zeros((nseq * nch, 2 * LANES), F32)
        for t in range(CMP_STRIDE // 2):
            lhs = jnp.concatenate(
                [jnp.concatenate([x[j * per_page:(j + 1) * per_page, c0:c0 + LANES] for j in (2 * t, 2 * t + 1)], axis=1)
                 for x in xs], axis=0)
            acc = acc + _dot(lhs.astype(BF16), w1[t])
        c = _cmp_finish(acc, b[...], w2[...], sbuf).astype(BF16)
        return [c[sq * nch:(sq + 1) * nch, :] for sq in seqs]

    kcs = compress(0, wk_ref, bk_ref, w2k_ref)
    vcs = compress(LANES, wv_ref, bv_ref, w2v_ref)

    ns = (past + 1 + SEL_BLOCK - 1) // SEL_BLOCK
    q_blk = qpos // SEL_BLOCK
    w_len = win_ref.shape[2]
    lane = lax.broadcasted_iota(jnp.int32, (R, LANES), 1)
    rows_of = lambda x, sq: x[sq * N_HEADS:(sq + 1) * N_HEADS, :]
    per_row = lambda x: jnp.concatenate([jnp.broadcast_to(x[sq:sq + 1, :], (N_HEADS, x.shape[1])) for sq in seqs], axis=0)
    as_bf16 = lambda x: x.astype(BF16).astype(F32)
    qzs = [_stack_q(q_ref[sq:sq + 1, :]) for sq in seqs]
    qz = jnp.concatenate(qzs, axis=0)
    qf = qz.astype(F32)
    kv_new = per_row(kvp_ref[...])
    kvw_new = per_row(kvw_ref[...])

    s_c = jnp.concatenate([_dot_nt(qzs[sq], kcs[sq]) for sq in seqs], axis=0)
    blk = lax.broadcasted_iota(jnp.int32, (R, nch), 1)
    p_c = _masked_softmax(s_c, (blk * CMP_STRIDE + (CMP_LEN - 1) <= qpos) & (blk < nch - 1))
    p_cb = p_c.astype(BF16)
    o_c = jnp.concatenate([_dot(rows_of(p_cb, sq), vcs[sq]) for sq in seqs], axis=0)

    psum = jnp.concatenate(
        [jnp.sum(p_c[sq * N_HEADS + kvh * GROUP:sq * N_HEADS + (kvh + 1) * GROUP, :], axis=0, keepdims=True)
         if kvh < N_KV_HEADS else jnp.zeros((N_HEADS - N_KV_HEADS, nch), F32)
         for sq in seqs for kvh in range(N_KV_HEADS + 1)], axis=0)
    imp = _dot_exact_rhs(psum, ov_ref[...], 3)
    forced = (lane == 0) | (lane == q_blk) | (lane == q_blk - 1)
    imp = jnp.where(lane <= q_blk, jnp.where(forced, FORCE_SCORE, imp), -1.0)
    imp = jnp.where(lane < ns, imp, -2.0)
    imp_t = imp.T
    sub = lax.broadcasted_iota(jnp.int32, (LANES, LANES), 0)
    lan = lax.broadcasted_iota(jnp.int32, (LANES, LANES), 1)
    pens = []
    for sq in seqs:
        for kvh in range(N_KV_HEADS):
            r = sq * N_HEADS + kvh
            other = imp_t[:, r:r + 1]
            mine = imp[r:r + 1, :]
            beats = (other > mine) | ((other == mine) & (sub < lan))
            cnt = jnp.sum(beats.astype(jnp.int32), axis=0, keepdims=True)
            sel = (cnt < N_SEL) & (lane[0:1, :] <= q_blk)
            pens += [jnp.where(sel | (lane[0:1, :] >= ns), 0.0, NEG_INF)] * GROUP
    pen = jnp.concatenate(pens, axis=0)
    q_aug = jnp.concatenate([qz, pen.astype(BF16)], axis=1)

    s_s = jnp.concatenate(
        [jnp.concatenate(
            [_dot(rows_of(q_aug, sq), jnp.concatenate([pg[2 * LANES:3 * LANES, :].astype(BF16),
                                                       oh_ref[:, p * PAGE_SIZE:(p + 1) * PAGE_SIZE]], axis=0))
             for p, pg in enumerate(pages[sq * n_pages:(sq + 1) * n_pages])], axis=1)
         for sq in seqs], axis=0)
    pen_new = jnp.sum(jnp.where(lane == q_blk, pen, 0.0), axis=-1, keepdims=True)
    s_new = jnp.sum(qf * as_bf16(kv_new[:, 2 * LANES:3 * LANES]), axis=-1, keepdims=True) + pen_new
    m = jnp.maximum(jnp.max(s_s, axis=-1, keepdims=True), s_new)
    p_s = jnp.exp2(s_s - m)
    p_new = jnp.exp2(s_new - m)
    den = jnp.sum(p_s, axis=-1, keepdims=True) + p_new
    p_sb = p_s.astype(BF16)
    pv = []
    for sq in seqs:
        o = jnp.zeros((N_HEADS, LANES), F32)
        for p, pg in enumerate(pages[sq * n_pages:(sq + 1) * n_pages]):
            o = o + _dot_nt(rows_of(p_sb, sq)[:, p * PAGE_SIZE:(p + 1) * PAGE_SIZE], pg[3 * LANES:4 * LANES, :].astype(BF16))
        pv.append(o)
    o_s = (jnp.concatenate(pv, axis=0) + as_bf16(p_new) * as_bf16(kv_new[:, 3 * LANES:4 * LANES])) / den

    wins = [win_ref[sq] for sq in seqs]
    s_w = jnp.concatenate([_dot(qzs[sq], wins[sq][:LANES, :].astype(BF16)) for sq in seqs], axis=0)
    diff = qpos - (past - w_len + lax.broadcasted_iota(jnp.int32, (R, w_len), 1))
    ok = (diff >= 0) & (diff < WINDOW)
    s_wn = jnp.sum(qf * as_bf16(kvw_new[:, :LANES]), axis=-1, keepdims=True)
    m = jnp.maximum(jnp.max(jnp.where(ok, s_w, NEG_INF), axis=-1, keepdims=True), s_wn)
    p_w = jnp.where(ok, jnp.exp2(s_w - m), 0.0)
    p_wn = jnp.exp2(s_wn - m)
    den = jnp.sum(p_w, axis=-1, keepdims=True) + p_wn
    p_wb = p_w.astype(BF16)
    o_w = (jnp.concatenate([_dot_nt(rows_of(p_wb, sq), wins[sq][LANES:, :].astype(BF16)) for sq in seqs], axis=0)
           + as_bf16(p_wn) * as_bf16(kvw_new[:, LANES:])) / den

    o_c, o_s, o_w = (jnp.concatenate([_unstack_o(rows_of(o, sq), 1) for sq in seqs], axis=0) for o in (o_c, o_s, o_w))
    out_ref[...] = _combine(gates_ref[...], ex_ref[...], o_c, o_s, o_w, go_ref[...])

    eye_w = (lax.broadcasted_iota(jnp.int32, (2 * LANES, 2 * LANES), 0)
             == lax.broadcasted_iota(jnp.int32, (2 * LANES, 2 * LANES), 1))
    last_lane = lax.broadcasted_iota(jnp.int32, (2 * LANES, w_len), 1) == w_len - 1
    for sq in seqs:
        new_col = jnp.sum(jnp.where(eye_w, kvw_ref[sq:sq + 1, :], 0.0), axis=1, keepdims=True)
        wout_ref[sq] = jnp.where(last_lane, new_col, pltpu.roll(wins[sq], w_len - 1, 1))


def _sample_attn(q, kvp, kvw, gates, cache_kv_t, cache_win_t, page_table, layer, w, consts, nseq=4):
    db, n_pages = page_table.shape
    w_len = cache_win_t.shape[3]
    nch = n_pages * PAGE_SIZE // CMP_STRIDE
    assert db % nseq == 0
    r3 = lambda a: a.reshape(db // nseq, nseq, a.shape[-1])
    rowspec = lambda c: pl.BlockSpec((None, nseq, c), lambda b, pt: (b, 0, 0))
    whole = lambda a: pl.BlockSpec(a.shape, lambda b, pt: (0,) * a.ndim)
    wspec = lambda a: pl.BlockSpec((None,) + a.shape[1:], lambda b, pt: (layer,) + (0,) * (a.ndim - 1))
    page_specs = [pl.BlockSpec((None, None, N_PAGED * LANES, PAGE_SIZE),
                               functools.partial(lambda sq, p, b, pt: (layer, pt[b * nseq + sq, p], 0, 0), sq, p))
                  for sq in range(nseq) for p in range(n_pages)]
    ws = [w["cmp_w1_k"], w["cmp_w1_v"], w["cmp_b_k"], w["cmp_b_v"], w["cmp_w2_k"], w["cmp_w2_v"]]
    cs = [consts["perm"], consts["onehot_s"], consts["ov_s"], consts["expand"]]
    grid_spec = pltpu.PrefetchScalarGridSpec(
        num_scalar_prefetch=1, grid=(db // nseq,),
        in_specs=[rowspec(ATTN_DIM), rowspec(N_PAGED * LANES), rowspec(2 * LANES), rowspec(GATE_PAD)] + page_specs
        + [pl.BlockSpec((None, nseq, 2 * LANES, w_len), lambda b, pt: (layer, b, 0, 0))]
        + [wspec(a) for a in ws] + [whole(a) for a in cs] + [wspec(w["g_attn_out"])],
        out_specs=[rowspec(ATTN_DIM), pl.BlockSpec((nseq, 2 * LANES, w_len), lambda b, pt: (b, 0, 0))],
        scratch_shapes=[pltpu.VMEM((nseq * nch + 8, LANES), F32)],
    )
    out, wout = pl.pallas_call(
        functools.partial(_sample_attn_body, n_pages, nseq),
        grid_spec=grid_spec,
        out_shape=[jax.ShapeDtypeStruct((db // nseq, nseq, ATTN_DIM), BF16),
                   jax.ShapeDtypeStruct((db, 2 * LANES, w_len), F32)],
        compiler_params=_params("arbitrary"), name="sample_attn",
    )(page_table, r3(q), r3(kvp), r3(kvw), r3(gates), *([cache_kv_t] * (nseq * n_pages)), cache_win_t, *ws, *cs,
      w["g_attn_out"])
    return out.reshape(db, ATTN_DIM), wout


def _postmix_body(final, x_ref, yc_ref, ya_ref, woc_ref, woa_ref, gf_ref, wg_ref, wu_ref, wd_ref, gl_ref, xo_ref, *rest):
    h_s, acc_s = rest[-2:]
    f = pl.program_id(1)

    @pl.when(f == 0)
    def _():
        x1 = x_ref[...] + _dot(yc_ref[...], woc_ref[...]) + _dot(ya_ref[...], woa_ref[...])
        h_s[...] = _rms(x1, gf_ref[...]).astype(BF16)
        acc_s[...] = x1

    h = h_s[...]
    a = _dot(h, wg_ref[...])
    act = (a * jax.nn.sigmoid(a) * _dot(h, wu_ref[...])).astype(BF16)
    acc_s[...] += _dot(act, wd_ref[...])

    @pl.when(f == pl.num_programs(1) - 1)
    def _():
        out = acc_s[...]
        xo_ref[...] = out
        if final:
            rest[0][...] = _rms(out, gl_ref[...])


def _postmix(x, ycn, yan, layer, w, final, tm=512, n_ff=2):
    t, d = x.shape
    tm = min(tm, t)
    d_ff = w["w_gate"].shape[-1]
    tf = d_ff // n_ff
    assert tf * n_ff == d_ff and tf % LANES == 0
    row = lambda c: pl.BlockSpec((tm, c), lambda i, f: (i, 0))
    wspec = lambda a: pl.BlockSpec((None,) + a.shape[1:], lambda i, f: (layer,) + (0,) * (a.ndim - 1))
    gl = w["g_final"]
    n_out = 2 if final else 1
    outs = pl.pallas_call(
        functools.partial(_postmix_body, final),
        grid=(t // tm, n_ff),
        in_specs=[row(d), row(CONV_DIM), row(ATTN_DIM), wspec(w["w_out_c"]), wspec(w["w_out_a"]), wspec(w["g_ffn"]),
                  pl.BlockSpec((None, d, tf), lambda i, f: (layer, 0, f)),
                  pl.BlockSpec((None, d, tf), lambda i, f: (layer, 0, f)),
                  pl.BlockSpec((None, tf, d), lambda i, f: (layer, f, 0)),
                  pl.BlockSpec(gl.shape, lambda i, f: (0, 0))],
        out_specs=[row(d)] * n_out,
        out_shape=[jax.ShapeDtypeStruct((t, d), F32)] * n_out,
        scratch_shapes=[pltpu.VMEM((tm, d), BF16), pltpu.VMEM((tm, d), F32)],
        compiler_params=_params("arbitrary", "arbitrary"), name="postmix",
    )(x, ycn, yan, w["w_out_c"], w["w_out_a"], w["g_ffn"], w["w_gate"], w["w_up"], w["w_down"], gl)
    return outs


def _prep_weights(g_mix, w_in, conv_w, cmp_pe_k, cmp_w1_k, cmp_w2_k, cmp_pe_v, cmp_w1_v, cmp_w2_v, g_conv_out,
                  g_attn_out, w_out, g_ffn, w_gate, w_up, w_down, g_final):
    nl, d, _ = w_in.shape
    c3 = 3 * CONV_DIM
    kv_cols = 6 * LANES

    def perm_heads(a, axis):
        shp = a.shape
        a = a.reshape(shp[:axis] + (N_KV_HEADS, GROUP, HEAD_DIM) + shp[axis + 1:])
        a = jnp.swapaxes(a, axis, axis + 1)
        return a.reshape(shp)

    def blockdiag2(a):
        z = jnp.zeros_like(a)
        return jnp.concatenate([jnp.concatenate([a, z], axis=-1), jnp.concatenate([z, a], axis=-1)], axis=-2)

    def cmp_w1(w1):
        a = blockdiag2(w1[:, :CMP_STRIDE])
        b = blockdiag2(w1[:, CMP_STRIDE:])
        return jnp.concatenate([a, b], axis=-1).astype(BF16).reshape(nl, CMP_STRIDE // 2, 2 * LANES, 2 * LANES)

    pe = jnp.stack([cmp_pe_k, cmp_pe_v], axis=1).reshape(nl, 2, CMP_LEN * HEAD_DIM, 1)
    w1 = jnp.stack([cmp_w1_k, cmp_w1_v], axis=1).reshape(nl, 2, CMP_LEN * HEAD_DIM, HEAD_DIM)
    bias = _cmp_bias(pe, w1)
    bias2 = jnp.concatenate([bias, bias], axis=-1)
    gates_w = jnp.pad(w_in[:, :, c3 + ATTN_DIM + kv_cols:], ((0, 0), (0, 0), (0, GATE_PAD - 3 * N_HEADS)))
    return {
        "g_mix": g_mix.reshape(nl, 1, d),
        "w_c": w_in[:, :, :c3].astype(BF16),
        "w_q": perm_heads(w_in[:, :, c3:c3 + ATTN_DIM], 2).astype(BF16),
        "w_kv": w_in[:, :, c3 + ATTN_DIM:c3 + ATTN_DIM + kv_cols].astype(BF16),
        "w_g": gates_w.astype(BF16),
        "conv_w": conv_w,
        "g_conv_out": g_conv_out.reshape(nl, 1, CONV_DIM),
        "cmp_w1_k": cmp_w1(cmp_w1_k), "cmp_w1_v": cmp_w1(cmp_w1_v),
        "cmp_b_k": bias2[:, 0], "cmp_b_v": bias2[:, 1],
        "cmp_w2_k": blockdiag2(cmp_w2_k).astype(BF16), "cmp_w2_v": blockdiag2(cmp_w2_v).astype(BF16),
        "g_attn_out": perm_heads(g_attn_out, 1).reshape(nl, 1, ATTN_DIM),
        "w_out_c": w_out[:, :CONV_DIM].astype(BF16),
        "w_out_a": perm_heads(w_out[:, CONV_DIM:], 1).astype(BF16),
        "g_ffn": g_ffn.reshape(nl, 1, d),
        "w_gate": w_gate.astype(BF16), "w_up": w_up.astype(BF16), "w_down": w_down.astype(BF16),
        "g_final": g_final.reshape(1, d),
    }


def kernel(x_prompt, x_sample, cache_kv, cache_win, state_conv, page_table, g_mix, w_in, conv_w, cmp_pe_k, cmp_w1_k, cmp_w2_k, cmp_pe_v, cmp_w1_v, cmp_w2_v, g_conv_out, g_attn_out, w_out, g_ffn, w_gate, w_up, w_down, g_final):
    nl = w_in.shape[0]
    b, s, d = x_prompt.shape
    db, sq = x_sample.shape[:2]
    assert sq == 1 and s % 512 == 0
    n_pages = page_table.shape[1]
    past = n_pages * PAGE_SIZE
    w_len = cache_win.shape[2]
    w_keep = min(WINDOW, s)
    w = _prep_weights(g_mix, w_in, conv_w, cmp_pe_k, cmp_w1_k, cmp_w2_k, cmp_pe_v, cmp_w1_v, cmp_w2_v, g_conv_out,
                      g_attn_out, w_out, g_ffn, w_gate, w_up, w_down, g_final)
    consts = {
        "expand": _gate_expand(),
        "onehot": _sel_onehot(s), "ov": _cmp_to_sel(s // CMP_STRIDE, s // SEL_BLOCK),
        "onehot_s": _sel_onehot(past).T, "ov_s": _cmp_to_sel(past // CMP_STRIDE, (past + 1 + SEL_BLOCK - 1) // SEL_BLOCK),
        "perm": _chunk_perm(),
    }
    cache_kv = cache_kv.transpose(0, 1, 3, 4, 5, 2).reshape(nl, cache_kv.shape[1], N_PAGED * LANES, PAGE_SIZE)
    cache_win = cache_win.transpose(0, 1, 3, 4, 5, 2).reshape(nl, db, 2 * LANES, w_len)

    xp, xs = x_prompt, x_sample
    kvp_l, winp_l, convp_l, kvs_l, wins_l, convs_l = [], [], [], [], [], []
    for l in range(nl):
        final = l == nl - 1
        ycn, q, kvp, kvw, kvb, gates, tail = _premix(xp, l, w)
        kc, vc = _compress(kvp, l, w)
        yan = _prompt_attn(q, kc, vc, kvb, gates, l, w, consts)
        outs = _postmix(xp.reshape(b * s, d), ycn.reshape(b * s, -1), yan.reshape(b * s, -1), l, w, final)
        xp = outs[0].reshape(b, s, d)
        if final:
            y_prompt = outs[1].reshape(b, s, d)
        kvp_l.append(kvp)
        winp_l.append(kvw[:, s - w_keep:])
        convp_l.append(tail[:, 8 - (CONV_WIDTH - 1):])
        ycn, q, kvp, kvw, kvb, gates, u = _premix(xs, l, w, prev=(state_conv[l, :, 1], state_conv[l, :, 0]))
        yan, wout = _sample_attn(q, kvp, kvw, gates, cache_kv, cache_win, page_table, l, w, consts)
        outs = _postmix(xs.reshape(db, d), ycn, yan, l, w, final)
        xs = outs[0].reshape(db, 1, d)
        if final:
            y_sample = outs[1].reshape(db, 1, d)
        kvs_l.append(kvp)
        wins_l.append(wout)
        convs_l.append(jnp.stack([state_conv[l, :, 1], u], axis=1))
    kv_shape = (N_PAGED, N_KV_HEADS, HEAD_DIM)
    return (y_prompt, y_sample,
            jnp.stack(kvp_l).reshape((nl, b, s) + kv_shape),
            jnp.stack(winp_l).reshape((nl, b, w_keep, 2) + kv_shape[1:]),
            jnp.stack(convp_l),
            jnp.stack(kvs_l).reshape((nl, db, 1) + kv_shape),
            jnp.stack(wins_l).reshape((nl, db, 2) + kv_shape[1:] + (w_len,)).transpose(0, 1, 5, 2, 3, 4),
            jnp.stack(convs_l))
```

```python
import functools

import numpy as np
import jax
import jax.numpy as jnp
from jax import lax
from jax.experimental import pallas as pl
from jax.experimental.pallas import tpu as pltpu

HEAD_DIM = 64
N_KV_HEADS = 2
GROUP = 4
N_HEADS = N_KV_HEADS * GROUP
LANES = 2 * HEAD_DIM
CONV_DIM = 512
ATTN_DIM = N_HEADS * HEAD_DIM
CONV_WIDTH = 3
CMP_STRIDE = 16
CMP_LEN = 2 * CMP_STRIDE
SEL_BLOCK = 64
N_SEL = 16
WINDOW = 512
Q_BLOCK = 128
PAGE_SIZE = 128
N_PAGED = 4
RMS_EPS = 1e-6
NEG_INF = -1e30
FORCE_SCORE = 1e4
SCALE = HEAD_DIM ** -0.5
Q_SCALE = SCALE * 1.4426950408889634
GATE_PAD = 128
KEY_CHUNK = 512
VMEM_LIMIT = 56 * 1024 * 1024

F32 = jnp.float32
BF16 = jnp.bfloat16


def _rms(x, g):
    return x * lax.rsqrt(jnp.mean(x * x, axis=-1, keepdims=True) + RMS_EPS) * g


def _dot(a, b):
    return jnp.dot(a, b, preferred_element_type=F32)


def _dot_nt(a, b):
    return lax.dot_general(a, b, (((1,), (1,)), ((), ())), preferred_element_type=F32)


def _halves(dot, a, b):
    h = a.shape[0] // 2
    return jnp.concatenate([dot(a[:h], b), dot(a[h:], b)], axis=0)


def _params(*sem):
    return pltpu.CompilerParams(dimension_semantics=sem, vmem_limit_bytes=VMEM_LIMIT)


def _premix_body(seq_mode, tm, x_ref, g_ref, wc_ref, wq_ref, wkv_ref, wg_ref, cw_ref, gco_ref, *rest):
    if seq_mode:
        ycn_ref, q_ref, kvp_ref, kvw_ref, kvb_ref, gates_ref, tail_ref, ubuf = rest
    else:
        p1_ref, p2_ref, ycn_ref, q_ref, kvp_ref, kvw_ref, kvb_ref, gates_ref, tail_ref = rest
    x = x_ref[...]
    xn = _rms(x, g_ref[...]).astype(BF16)
    pc = _dot(xn, wc_ref[...])
    bg, u = pc[:, :CONV_DIM], pc[:, CONV_DIM:2 * CONV_DIM] * pc[:, 2 * CONV_DIM:]
    if seq_mode:
        @pl.when(pl.program_id(1) == 0)
        def _():
            ubuf[0:8, :] = jnp.zeros((8, CONV_DIM), F32)
        ubuf[8:8 + tm, :] = u
        up1 = ubuf[7:7 + tm, :]
        up2 = ubuf[6:6 + tm, :]
        ubuf[0:8, :] = ubuf[tm:tm + 8, :]
        tail_ref[...] = u[tm - 8:, :]
    else:
        up1, up2 = p1_ref[...], p2_ref[...]
        tail_ref[...] = u
    cw = cw_ref[...]
    yc = bg * (cw[0:1, :] * up2 + cw[1:2, :] * up1 + cw[2:3, :] * u)
    ycn_ref[...] = _rms(yc, gco_ref[...]).astype(BF16)
    q_ref[...] = (_dot(xn, wq_ref[...]) * Q_SCALE).astype(BF16)
    kv = _dot(xn, wkv_ref[...])
    kvp_ref[...] = kv[:, :N_PAGED * LANES]
    kvw_ref[...] = kv[:, N_PAGED * LANES:]
    kvb_ref[...] = kv[:, 2 * LANES:].astype(BF16)
    gates_ref[...] = jax.nn.sigmoid(_dot(xn, wg_ref[...]))


def _premix(x, layer, w, prev=None, tm=512):
    nb, s, d = x.shape
    seq_mode = prev is None
    if seq_mode:
        grid = (nb, s // tm)
        row = lambda c: pl.BlockSpec((None, tm, c), lambda b, i: (b, i, 0))
        tail_spec = pl.BlockSpec((None, 8, CONV_DIM), lambda b, i: (b, 0, 0))
        tail_shape = (nb, 8, CONV_DIM)
        xin = x
        lead = (nb, s)
    else:
        tm = nb
        grid = (1, 1)
        row = lambda c: pl.BlockSpec((tm, c), lambda b, i: (0, 0))
        tail_spec = row(CONV_DIM)
        tail_shape = (nb, CONV_DIM)
        xin = x.reshape(nb, d)
        lead = (nb,)
    wspec = lambda a: pl.BlockSpec((None,) + a.shape[1:], lambda b, i: (layer,) + (0,) * (a.ndim - 1))
    ws = [w["g_mix"], w["w_c"], w["w_q"], w["w_kv"], w["w_g"], w["conv_w"], w["g_conv_out"]]
    in_specs = [row(d)] + [wspec(a) for a in ws]
    args = [xin] + ws
    scratch = []
    if seq_mode:
        scratch = [pltpu.VMEM((tm + 8, CONV_DIM), F32)]
    else:
        in_specs += [row(CONV_DIM), row(CONV_DIM)]
        args += list(prev)
    out_shape = [
        jax.ShapeDtypeStruct(lead + (CONV_DIM,), BF16),
        jax.ShapeDtypeStruct(lead + (ATTN_DIM,), BF16),
        jax.ShapeDtypeStruct(lead + (N_PAGED * LANES,), F32),
        jax.ShapeDtypeStruct(lead + (2 * LANES,), F32),
        jax.ShapeDtypeStruct(lead + (4 * LANES,), BF16),
        jax.ShapeDtypeStruct(lead + (GATE_PAD,), F32),
        jax.ShapeDtypeStruct(tail_shape, F32),
    ]
    out_specs = [row(CONV_DIM), row(ATTN_DIM), row(N_PAGED * LANES), row(2 * LANES), row(4 * LANES),
                 row(GATE_PAD), tail_spec]
    return pl.pallas_call(
        functools.partial(_premix_body, seq_mode, tm),
        grid=grid, in_specs=in_specs, out_specs=out_specs, out_shape=out_shape,
        scratch_shapes=scratch, compiler_params=_params("arbitrary", "arbitrary"),
        name="premix_seq" if seq_mode else "premix_step",
    )(*args)


def _cmp_bias_body(pe_ref, w1_ref, o_ref):
    o_ref[...] = jnp.sum(pe_ref[...] * w1_ref[...], axis=0, keepdims=True)


def _cmp_bias(pe, w1):
    nl = pe.shape[0]
    k = CMP_LEN * HEAD_DIM
    return pl.pallas_call(
        _cmp_bias_body,
        grid=(nl, 2),
        in_specs=[pl.BlockSpec((None, None, k, 1), lambda l, t: (l, t, 0, 0)),
                  pl.BlockSpec((None, None, k, HEAD_DIM), lambda l, t: (l, t, 0, 0))],
        out_specs=pl.BlockSpec((None, None, 1, HEAD_DIM), lambda l, t: (l, t, 0, 0)),
        out_shape=jax.ShapeDtypeStruct((nl, 2, 1, HEAD_DIM), F32),
        compiler_params=_params("arbitrary", "arbitrary"), name="cmp_bias",
    )(pe, w1)


def _cmp_finish(acc, bias, w2, sbuf):
    n = acc.shape[0]
    sbuf[0:n, :] = acc[:, LANES:]
    sbuf[n:n + 8, :] = jnp.zeros((8, LANES), F32)
    h = jax.nn.gelu(acc[:, :LANES] + sbuf[1:n + 1, :] + bias)
    return _dot(h.astype(BF16), w2)


def _compress_body(nch, k_ref, v_ref, wk_ref, wv_ref, bk_ref, bv_ref, w2k_ref, w2v_ref, kc_ref, vc_ref, sbuf):
    for src, w1, b, w2, out in ((k_ref, wk_ref, bk_ref, w2k_ref, kc_ref), (v_ref, wv_ref, bv_ref, w2v_ref, vc_ref)):
        acc = jnp.zeros((nch, 2 * LANES), F32)
        for t in range(CMP_STRIDE // 2):
            lhs = jnp.concatenate([src[pl.ds(j, nch, stride=CMP_STRIDE), :] for j in (2 * t, 2 * t + 1)], axis=1)
            acc = acc + _dot(lhs.astype(BF16), w1[t])
        out[...] = _cmp_finish(acc, b[...], w2[...], sbuf).astype(BF16)


def _compress(kvp, layer, w):
    nb, s, _ = kvp.shape
    nch = s // CMP_STRIDE
    wspec = lambda a: pl.BlockSpec((None,) + a.shape[1:], lambda b: (layer,) + (0,) * (a.ndim - 1))
    ws = [w["cmp_w1_k"], w["cmp_w1_v"], w["cmp_b_k"], w["cmp_b_v"], w["cmp_w2_k"], w["cmp_w2_v"]]
    return pl.pallas_call(
        functools.partial(_compress_body, nch),
        grid=(nb,),
        in_specs=[pl.BlockSpec((None, s, LANES), lambda b: (b, 0, 0)),
                  pl.BlockSpec((None, s, LANES), lambda b: (b, 0, 1))] + [wspec(a) for a in ws],
        out_specs=[pl.BlockSpec((None, nch, LANES), lambda b: (b, 0, 0))] * 2,
        out_shape=[jax.ShapeDtypeStruct((nb, nch, LANES), BF16)] * 2,
        scratch_shapes=[pltpu.VMEM((nch + 8, LANES), F32)],
        compiler_params=_params("arbitrary"), name="compress",
    )(kvp, kvp, *ws)


def _half_mask(rows):
    return lax.broadcasted_iota(jnp.int32, (rows, LANES), 1) < HEAD_DIM


def _stack_q(qt):
    n = qt.shape[0]
    lo = _half_mask(n)
    zero = jnp.zeros((n, LANES), qt.dtype)
    parts = []
    for kvh in range(N_KV_HEADS):
        for g in range(GROUP):
            blk = qt[:, g * LANES:(g + 1) * LANES]
            parts.append(jnp.where(lo if kvh == 0 else ~lo, blk, zero))
    return jnp.concatenate(parts, axis=0)


def _unstack_o(o, n):
    lo = _half_mask(n)
    cols = [jnp.where(lo, o[g * n:(g + 1) * n, :], o[(GROUP + g) * n:(GROUP + g + 1) * n, :]) for g in range(GROUP)]
    return jnp.concatenate(cols, axis=1)


def _masked_softmax(s, mask):
    m = jnp.max(jnp.where(mask, s, NEG_INF), axis=-1, keepdims=True)
    e = jnp.where(mask, jnp.exp2(s - m), 0.0)
    l = jnp.sum(e, axis=-1, keepdims=True)
    return e / jnp.where(l > 0.0, l, 1.0)


def _add_tile_bias(s, bias):
    n = bias.shape[0]
    return jnp.concatenate([s[r * n:(r + 1) * n, :] + bias for r in range(s.shape[0] // n)], axis=0)


def _split_bf16(x, terms):
    out = []
    for _ in range(terms):
        hi = x.astype(BF16)
        out.append(hi)
        x = x - hi.astype(F32)
    return out


def _dot_exact_rhs(x, rhs_bf16, terms):
    return sum(_dot(t, rhs_bf16) for t in _split_bf16(x, terms))


def _combine(gates, expand, o_c, o_s, o_w, g_out):
    gx = _dot_exact_rhs(gates, expand, 2)
    ya = gx[:, :ATTN_DIM] * o_c + gx[:, ATTN_DIM:2 * ATTN_DIM] * o_s + gx[:, 2 * ATTN_DIM:] * o_w
    return _rms(ya, g_out).astype(BF16)


def _cmp_to_sel(nc_rows, ns):
    n = np.arange(nc_rows)[:, None]
    s = np.arange(LANES)[None, :]
    ov = (n * CMP_STRIDE <= s * SEL_BLOCK + SEL_BLOCK - 1) & (n * CMP_STRIDE + CMP_LEN - 1 >= s * SEL_BLOCK) & (s < ns)
    return jnp.asarray(ov.astype(np.float32), dtype=BF16)


def _gate_expand():
    e = np.zeros((GATE_PAD, 3 * ATTN_DIM), np.float32)
    for br in range(3):
        for kvh in range(N_KV_HEADS):
            for g in range(GROUP):
                c0 = br * ATTN_DIM + g * LANES + kvh * HEAD_DIM
                e[br * N_HEADS + kvh * GROUP + g, c0:c0 + HEAD_DIM] = 1.0
    return jnp.asarray(e, dtype=BF16)


def _sel_onehot(n_keys):
    t = np.arange(n_keys)[:, None] // SEL_BLOCK
    return jnp.asarray((t == np.arange(LANES)[None, :]).astype(np.float32), dtype=BF16)


def _chunk_perm():
    m = np.arange(PAGE_SIZE)[:, None]
    t = np.arange(PAGE_SIZE)[None, :]
    per_page = PAGE_SIZE // CMP_STRIDE
    return jnp.asarray((t == CMP_STRIDE * (m % per_page) + m // per_page).astype(np.float32), dtype=BF16)


def _rank_penalty(imp_t, ns):
    r = imp_t.shape[1]
    sub = lax.broadcasted_iota(jnp.int32, (8, r), 0)
    cnt = jnp.zeros((ns, r), F32)
    for sp in range(ns):
        other = imp_t[sp:sp + 1, :]
        parts = []
        for blk in range(ns // 8):
            t = imp_t[blk * 8:(blk + 1) * 8, :]
            if blk * 8 > sp:
                parts.append(jnp.where(other >= t, 1.0, 0.0))
            elif blk * 8 + 7 <= sp:
                parts.append(jnp.where(other > t, 1.0, 0.0))
            else:
                parts.append(jnp.where(sub > sp - blk * 8, jnp.where(other >= t, 1.0, 0.0), jnp.where(other > t, 1.0, 0.0)))
        cnt = cnt + jnp.concatenate(parts, axis=0)
    return jnp.where(cnt < N_SEL, 0.0, NEG_INF)


def _prompt_attn_body(ns, q_ref, kc_ref, vc_ref, ksl_ref, vsl_ref, kw_ref, vw_ref, oh_ref, ov_ref, gates_ref,
                      ex_ref, go_ref, out_ref, pen_s):
    i = pl.program_id(1)
    nq = Q_BLOCK
    rows = N_HEADS * nq
    qz = _stack_q(q_ref[...])
    qpos_t = i * nq + lax.broadcasted_iota(jnp.int32, (nq, 1), 0)

    kc = kc_ref[...]
    nc = kc.shape[0]
    c_end = lax.broadcasted_iota(jnp.int32, (nq, nc), 1) * CMP_STRIDE + (CMP_LEN - 1)
    s_c = _add_tile_bias(_halves(_dot_nt, qz, kc), jnp.where(c_end <= qpos_t, 0.0, NEG_INF))
    e_c = jnp.exp2(s_c - jnp.max(s_c, axis=-1, keepdims=True))
    any_c = jnp.where(qpos_t >= CMP_LEN - 1, 1.0, 0.0)
    p_c = e_c * (jnp.concatenate([any_c] * N_HEADS, axis=0) / jnp.sum(e_c, axis=-1, keepdims=True))
    o_c = _dot(p_c.astype(BF16), vc_ref[...])

    psum = jnp.concatenate(
        [sum(p_c[(kvh * GROUP + g) * nq:(kvh * GROUP + g + 1) * nq, :] for g in range(GROUP))
         for kvh in range(N_KV_HEADS)], axis=0)
    imp = _dot_exact_rhs(psum, ov_ref[...], 3)
    r2 = N_KV_HEADS * nq
    s_idx = lax.broadcasted_iota(jnp.int32, (r2, LANES), 1)
    q_blk = (i * nq + lax.broadcasted_iota(jnp.int32, (r2, LANES), 0) % nq) // SEL_BLOCK
    forced = (s_idx == 0) | (s_idx == q_blk) | (s_idx == q_blk - 1)
    imp = jnp.where(s_idx <= q_blk, jnp.where(forced, FORCE_SCORE, imp), -1.0)
    imp = jnp.where(s_idx < ns, imp, -2.0)
    pen_s[...] = jnp.zeros((LANES, r2), F32)

    @pl.when((i + 1) * nq > N_SEL * SEL_BLOCK)
    def _():
        pen_s[0:ns, :] = _rank_penalty(imp.T[0:ns, :], ns)

    pen = pen_s[...].T
    pen = jnp.where(s_idx <= q_blk, pen, jnp.where(s_idx < ns, NEG_INF, 0.0)).astype(BF16)
    pen_rows = jnp.concatenate([pen[kvh * nq:(kvh + 1) * nq, :] for kvh in range(N_KV_HEADS) for _ in range(GROUP)],
                               axis=0)
    q_aug = jnp.concatenate([qz, pen_rows], axis=1)

    ones_v = jnp.ones((KEY_CHUNK, LANES), BF16)

    def scores(c):
        k0 = pl.multiple_of(c * KEY_CHUNK, KEY_CHUNK)
        k_aug = jnp.concatenate([ksl_ref[pl.ds(k0, KEY_CHUNK), :], oh_ref[pl.ds(k0, KEY_CHUNK), :]], axis=1)
        return _halves(_dot_nt, q_aug, k_aug).astype(BF16)

    def update(c, s, m_old, acc):
        k0 = pl.multiple_of(c * KEY_CHUNK, KEY_CHUNK)
        m_new = jnp.maximum(m_old, jnp.max(s, axis=-1, keepdims=True).astype(F32))
        p = jnp.exp2(s - m_new.astype(BF16))
        v_aug = jnp.concatenate([vsl_ref[pl.ds(k0, KEY_CHUNK), :], ones_v], axis=1)
        return m_new, jnp.exp2(m_old - m_new) * acc + _dot(p, v_aug)

    last = (i * nq) // KEY_CHUNK
    carry = (jnp.full((rows, 1), NEG_INF, F32), jnp.zeros((rows, 2 * LANES), F32))
    m_old, acc = lax.fori_loop(0, last, lambda c, cr: update(c, scores(c), *cr), carry)
    kpos = last * KEY_CHUNK + lax.broadcasted_iota(jnp.int32, (nq, KEY_CHUNK), 1)
    _, acc = update(last, _add_tile_bias(scores(last), jnp.where(kpos <= qpos_t, 0.0, NEG_INF).astype(BF16)), m_old, acc)
    o_s = acc[:, :LANES] / acc[:, LANES:]

    nw = WINDOW + nq
    w0 = pl.multiple_of(jnp.maximum(i * nq - WINDOW, 0), nq)
    diff = qpos_t - (w0 + lax.broadcasted_iota(jnp.int32, (nq, nw), 1))
    s_w = _add_tile_bias(_halves(_dot_nt, qz, kw_ref[pl.ds(w0, nw), :]).astype(BF16),
                         jnp.where((diff >= 0) & (diff < WINDOW), 0.0, NEG_INF).astype(BF16))
    e_w = jnp.exp2(s_w - jnp.max(s_w, axis=-1, keepdims=True))
    ow_aug = _dot(e_w, jnp.concatenate([vw_ref[pl.ds(w0, nw), :], jnp.ones((nw, LANES), BF16)], axis=1))
    o_w = ow_aug[:, :LANES] / ow_aug[:, LANES:]

    out_ref[...] = _combine(gates_ref[...], ex_ref[...], _unstack_o(o_c, nq), _unstack_o(o_s, nq),
                            _unstack_o(o_w, nq), go_ref[...])


def _prompt_attn(q, kc, vc, kvb, gates, layer, w, consts):
    nb, s, _ = q.shape
    nc = kc.shape[1]
    ns = s // SEL_BLOCK
    col = lambda c: pl.BlockSpec((None, s, LANES), lambda b, i: (b, 0, c))
    whole = lambda a: pl.BlockSpec(a.shape, lambda b, i: (0,) * a.ndim)
    return pl.pallas_call(
        functools.partial(_prompt_attn_body, ns),
        grid=(nb, s // Q_BLOCK),
        in_specs=[pl.BlockSpec((None, Q_BLOCK, ATTN_DIM), lambda b, i: (b, i, 0)),
                  pl.BlockSpec((None, nc, LANES), lambda b, i: (b, 0, 0)),
                  pl.BlockSpec((None, nc, LANES), lambda b, i: (b, 0, 0)),
                  col(0), col(1), col(2), col(3),
                  whole(consts["onehot"]), whole(consts["ov"]),
                  pl.BlockSpec((None, Q_BLOCK, GATE_PAD), lambda b, i: (b, i, 0)),
                  whole(consts["expand"]),
                  pl.BlockSpec((None, 1, ATTN_DIM), lambda b, i: (layer, 0, 0))],
        out_specs=pl.BlockSpec((None, Q_BLOCK, ATTN_DIM), lambda b, i: (b, i, 0)),
        out_shape=jax.ShapeDtypeStruct((nb, s, ATTN_DIM), BF16),
        scratch_shapes=[pltpu.VMEM((LANES, N_KV_HEADS * Q_BLOCK), F32)],
        compiler_params=_params("arbitrary", "arbitrary"), name="prompt_attn",
    )(q, kc, vc, kvb, kvb, kvb, kvb, consts["onehot"], consts["ov"], gates, consts["expand"], w["g_attn_out"])


def _sample_attn_body(n_pages, nseq, pt_ref, q_ref, kvp_ref, kvw_ref, gates_ref, *rest):
    n_pg = nseq * n_pages
    pages = rest[:n_pg]
    (win_ref, wk_ref, wv_ref, bk_ref, bv_ref, w2k_ref, w2v_ref, perm_ref, oh_ref, ov_ref, ex_ref, go_ref,
     out_ref, wout_ref, sbuf) = rest[n_pg:]
    past = n_pages * PAGE_SIZE
    qpos = past
    nch = past // CMP_STRIDE
    per_page = PAGE_SIZE // CMP_STRIDE
    seqs = range(nseq)
    R = nseq * N_HEADS
    perm = perm_ref[...]

    xs = [_dot_nt(perm, pg[0:2 * LANES, :].astype(BF16)) for pg in pages]

    def compress(c0, w1, b, w2):
        acc = jnp.zeros((nseq * nch, 2 * LANES), F32)
        for t in range(CMP_STRIDE // 2):
            lhs = jnp.concatenate(
                [jnp.concatenate([x[j * per_page:(j + 1) * per_page, c0:c0 + LANES] for j in (2 * t, 2 * t + 1)], axis=1)
                 for x in xs], axis=0)
            acc = acc + _dot(lhs.astype(BF16), w1[t])
        c = _cmp_finish(acc, b[...], w2[...], sbuf).astype(BF16)
        return [c[sq * nch:(sq + 1) * nch, :] for sq in seqs]

    kcs = compress(0, wk_ref, bk_ref, w2k_ref)
    vcs = compress(LANES, wv_ref, bv_ref, w2v_ref)

    ns = (past + 1 + SEL_BLOCK - 1) // SEL_BLOCK
    q_blk = qpos // SEL_BLOCK
    w_len = win_ref.shape[2]
    lane = lax.broadcasted_iota(jnp.int32, (R, LANES), 1)
    rows_of = lambda x, sq: x[sq * N_HEADS:(sq + 1) * N_HEADS, :]
    per_row = lambda x: jnp.concatenate([jnp.broadcast_to(x[sq:sq + 1, :], (N_HEADS, x.shape[1])) for sq in seqs], axis=0)
    as_bf16 = lambda x: x.astype(BF16).astype(F32)
    qzs = [_stack_q(q_ref[sq:sq + 1, :]) for sq in seqs]
    qz = jnp.concatenate(qzs, axis=0)
    qf = qz.astype(F32)
    kv_new = per_row(kvp_ref[...])
    kvw_new = per_row(kvw_ref[...])

    s_c = jnp.concatenate([_dot_nt(qzs[sq], kcs[sq]) for sq in seqs], axis=0)
    blk = lax.broadcasted_iota(jnp.int32, (R, nch), 1)
    p_c = _masked_softmax(s_c, (blk * CMP_STRIDE + (CMP_LEN - 1) <= qpos) & (blk < nch - 1))
    p_cb = p_c.astype(BF16)
    o_c = jnp.concatenate([_dot(rows_of(p_cb, sq), vcs[sq]) for sq in seqs], axis=0)

    psum = jnp.concatenate(
        [jnp.sum(p_c[sq * N_HEADS + kvh * GROUP:sq * N_HEADS + (kvh + 1) * GROUP, :], axis=0, keepdims=True)
         if kvh < N_KV_HEADS else jnp.zeros((N_HEADS - N_KV_HEADS, nch), F32)
         for sq in seqs for kvh in range(N_KV_HEADS + 1)], axis=0)
    imp = _dot_exact_rhs(psum, ov_ref[...], 3)
    forced = (lane == 0) | (lane == q_blk) | (lane == q_blk - 1)
    imp = jnp.where(lane <= q_blk, jnp.where(forced, FORCE_SCORE, imp), -1.0)
    imp = jnp.where(lane < ns, imp, -2.0)
    imp_t = imp.T
    sub = lax.broadcasted_iota(jnp.int32, (LANES, LANES), 0)
    lan = lax.broadcasted_iota(jnp.int32, (LANES, LANES), 1)
    pens = []
    for sq in seqs:
        for kvh in range(N_KV_HEADS):
            r = sq * N_HEADS + kvh
            other = imp_t[:, r:r + 1]
            mine = imp[r:r + 1, :]
            beats = (other > mine) | ((other == mine) & (sub < lan))
            cnt = jnp.sum(beats.astype(jnp.int32), axis=0, keepdims=True)
            sel = (cnt < N_SEL) & (lane[0:1, :] <= q_blk)
            pens += [jnp.where(sel | (lane[0:1, :] >= ns), 0.0, NEG_INF)] * GROUP
    pen = jnp.concatenate(pens, axis=0)
    q_aug = jnp.concatenate([qz, pen.astype(BF16)], axis=1)

    s_s = jnp.concatenate(
        [jnp.concatenate(
            [_dot(rows_of(q_aug, sq), jnp.concatenate([pg[2 * LANES:3 * LANES, :].astype(BF16),
                                                       oh_ref[:, p * PAGE_SIZE:(p + 1) * PAGE_SIZE]], axis=0))
             for p, pg in enumerate(pages[sq * n_pages:(sq + 1) * n_pages])], axis=1)
         for sq in seqs], axis=0)
    pen_new = jnp.sum(jnp.where(lane == q_blk, pen, 0.0), axis=-1, keepdims=True)
    s_new = jnp.sum(qf * as_bf16(kv_new[:, 2 * LANES:3 * LANES]), axis=-1, keepdims=True) + pen_new
    m = jnp.maximum(jnp.max(s_s, axis=-1, keepdims=True), s_new)
    p_s = jnp.exp2(s_s - m)
    p_new = jnp.exp2(s_new - m)
    den = jnp.sum(p_s, axis=-1, keepdims=True) + p_new
    p_sb = p_s.astype(BF16)
    pv = []
    for sq in seqs:
        o = jnp.zeros((N_HEADS, LANES), F32)
        for p, pg in enumerate(pages[sq * n_pages:(sq + 1) * n_pages]):
            o = o + _dot_nt(rows_of(p_sb, sq)[:, p * PAGE_SIZE:(p + 1) * PAGE_SIZE], pg[3 * LANES:4 * LANES, :].astype(BF16))
        pv.append(o)
    o_s = (jnp.concatenate(pv, axis=0) + as_bf16(p_new) * as_bf16(kv_new[:, 3 * LANES:4 * LANES])) / den

    wins = [win_ref[sq] for sq in seqs]
    s_w = jnp.concatenate([_dot(qzs[sq], wins[sq][:LANES, :].astype(BF16)) for sq in seqs], axis=0)
    diff = qpos - (past - w_len + lax.broadcasted_iota(jnp.int32, (R, w_len), 1))
    ok = (diff >= 0) & (diff < WINDOW)
    s_wn = jnp.sum(qf * as_bf16(kvw_new[:, :LANES]), axis=-1, keepdims=True)
    m = jnp.maximum(jnp.max(jnp.where(ok, s_w, NEG_INF), axis=-1, keepdims=True), s_wn)
    p_w = jnp.where(ok, jnp.exp2(s_w - m), 0.0)
    p_wn = jnp.exp2(s_wn - m)
    den = jnp.sum(p_w, axis=-1, keepdims=True) + p_wn
    p_wb = p_w.astype(BF16)
    o_w = (jnp.concatenate([_dot_nt(rows_of(p_wb, sq), wins[sq][LANES:, :].astype(BF16)) for sq in seqs], axis=0)
           + as_bf16(p_wn) * as_bf16(kvw_new[:, LANES:])) / den

    o_c, o_s, o_w = (jnp.concatenate([_unstack_o(rows_of(o, sq), 1) for sq in seqs], axis=0) for o in (o_c, o_s, o_w))
    out_ref[...] = _combine(gates_ref[...], ex_ref[...], o_c, o_s, o_w, go_ref[...])

    eye_w = (lax.broadcasted_iota(jnp.int32, (2 * LANES, 2 * LANES), 0)
             == lax.broadcasted_iota(jnp.int32, (2 * LANES, 2 * LANES), 1))
    last_lane = lax.broadcasted_iota(jnp.int32, (2 * LANES, w_len), 1) == w_len - 1
    for sq in seqs:
        new_col = jnp.sum(jnp.where(eye_w, kvw_ref[sq:sq + 1, :], 0.0), axis=1, keepdims=True)
        wout_ref[sq] = jnp.where(last_lane, new_col, pltpu.roll(wins[sq], w_len - 1, 1))


def _sample_attn(q, kvp, kvw, gates, cache_kv_t, cache_win_t, page_table, layer, w, consts, nseq=4):
    db, n_pages = page_table.shape
    w_len = cache_win_t.shape[3]
    nch = n_pages * PAGE_SIZE // CMP_STRIDE
    assert db % nseq == 0
    r3 = lambda a: a.reshape(db // nseq, nseq, a.shape[-1])
    rowspec = lambda c: pl.BlockSpec((None, nseq, c), lambda b, pt: (b, 0, 0))
    whole = lambda a: pl.BlockSpec(a.shape, lambda b, pt: (0,) * a.ndim)
    wspec = lambda a: pl.BlockSpec((None,) + a.shape[1:], lambda b, pt: (layer,) + (0,) * (a.ndim - 1))
    page_specs = [pl.BlockSpec((None, None, N_PAGED * LANES, PAGE_SIZE),
                               functools.partial(lambda sq, p, b, pt: (layer, pt[b * nseq + sq, p], 0, 0), sq, p))
                  for sq in range(nseq) for p in range(n_pages)]
    ws = [w["cmp_w1_k"], w["cmp_w1_v"], w["cmp_b_k"], w["cmp_b_v"], w["cmp_w2_k"], w["cmp_w2_v"]]
    cs = [consts["perm"], consts["onehot_s"], consts["ov_s"], consts["expand"]]
    grid_spec = pltpu.PrefetchScalarGridSpec(
        num_scalar_prefetch=1, grid=(db // nseq,),
        in_specs=[rowspec(ATTN_DIM), rowspec(N_PAGED * LANES), rowspec(2 * LANES), rowspec(GATE_PAD)] + page_specs
        + [pl.BlockSpec((None, nseq, 2 * LANES, w_len), lambda b, pt: (layer, b, 0, 0))]
        + [wspec(a) for a in ws] + [whole(a) for a in cs] + [wspec(w["g_attn_out"])],
        out_specs=[rowspec(ATTN_DIM), pl.BlockSpec((nseq, 2 * LANES, w_len), lambda b, pt: (b, 0, 0))],
        scratch_shapes=[pltpu.VMEM((nseq * nch + 8, LANES), F32)],
    )
    out, wout = pl.pallas_call(
        functools.partial(_sample_attn_body, n_pages, nseq),
        grid_spec=grid_spec,
        out_shape=[jax.ShapeDtypeStruct((db // nseq, nseq, ATTN_DIM), BF16),
                   jax.ShapeDtypeStruct((db, 2 * LANES, w_len), F32)],
        compiler_params=_params("arbitrary"), name="sample_attn",
    )(page_table, r3(q), r3(kvp), r3(kvw), r3(gates), *([cache_kv_t] * (nseq * n_pages)), cache_win_t, *ws, *cs,
      w["g_attn_out"])
    return out.reshape(db, ATTN_DIM), wout


def _postmix_body(final, x_ref, yc_ref, ya_ref, woc_ref, woa_ref, gf_ref, wg_ref, wu_ref, wd_ref, gl_ref, xo_ref, *rest):
    h_s, acc_s = rest[-2:]
    f = pl.program_id(1)

    @pl.when(f == 0)
    def _():
        x1 = x_ref[...] + _dot(yc_ref[...], woc_ref[...]) + _dot(ya_ref[...], woa_ref[...])
        h_s[...] = _rms(x1, gf_ref[...]).astype(BF16)
        acc_s[...] = x1

    h = h_s[...]
    a = _dot(h, wg_ref[...])
    act = (a * jax.nn.sigmoid(a) * _dot(h, wu_ref[...])).astype(BF16)
    acc_s[...] += _dot(act, wd_ref[...])

    @pl.when(f == pl.num_programs(1) - 1)
    def _():
        out = acc_s[...]
        xo_ref[...] = out
        if final:
            rest[0][...] = _rms(out, gl_ref[...])


def _postmix(x, ycn, yan, layer, w, final, tm=512, n_ff=2):
    t, d = x.shape
    tm = min(tm, t)
    d_ff = w["w_gate"].shape[-1]
    tf = d_ff // n_ff
    assert tf * n_ff == d_ff and tf % LANES == 0
    row = lambda c: pl.BlockSpec((tm, c), lambda i, f: (i, 0))
    wspec = lambda a: pl.BlockSpec((None,) + a.shape[1:], lambda i, f: (layer,) + (0,) * (a.ndim - 1))
    gl = w["g_final"]
    n_out = 2 if final else 1
    outs = pl.pallas_call(
        functools.partial(_postmix_body, final),
        grid=(t // tm, n_ff),
        in_specs=[row(d), row(CONV_DIM), row(ATTN_DIM), wspec(w["w_out_c"]), wspec(w["w_out_a"]), wspec(w["g_ffn"]),
                  pl.BlockSpec((None, d, tf), lambda i, f: (layer, 0, f)),
                  pl.BlockSpec((None, d, tf), lambda i, f: (layer, 0, f)),
                  pl.BlockSpec((None, tf, d), lambda i, f: (layer, f, 0)),
                  pl.BlockSpec(gl.shape, lambda i, f: (0, 0))],
        out_specs=[row(d)] * n_out,
        out_shape=[jax.ShapeDtypeStruct((t, d), F32)] * n_out,
        scratch_shapes=[pltpu.VMEM((tm, d), BF16), pltpu.VMEM((tm, d), F32)],
        compiler_params=_params("arbitrary", "arbitrary"), name="postmix",
    )(x, ycn, yan, w["w_out_c"], w["w_out_a"], w["g_ffn"], w["w_gate"], w["w_up"], w["w_down"], gl)
    return outs


def _prep_weights(g_mix, w_in, conv_w, cmp_pe_k, cmp_w1_k, cmp_w2_k, cmp_pe_v, cmp_w1_v, cmp_w2_v, g_conv_out,
                  g_attn_out, w_out, g_ffn, w_gate, w_up, w_down, g_final):
    nl, d, _ = w_in.shape
    c3 = 3 * CONV_DIM
    kv_cols = 6 * LANES

    def perm_heads(a, axis):
        shp = a.shape
        a = a.reshape(shp[:axis] + (N_KV_HEADS, GROUP, HEAD_DIM) + shp[axis + 1:])
        a = jnp.swapaxes(a, axis, axis + 1)
        return a.reshape(shp)

    def blockdiag2(a):
        z = jnp.zeros_like(a)
        return jnp.concatenate([jnp.concatenate([a, z], axis=-1), jnp.concatenate([z, a], axis=-1)], axis=-2)

    def cmp_w1(w1):
        a = blockdiag2(w1[:, :CMP_STRIDE])
        b = blockdiag2(w1[:, CMP_STRIDE:])
        return jnp.concatenate([a, b], axis=-1).astype(BF16).reshape(nl, CMP_STRIDE // 2, 2 * LANES, 2 * LANES)

    pe = jnp.stack([cmp_pe_k, cmp_pe_v], axis=1).reshape(nl, 2, CMP_LEN * HEAD_DIM, 1)
    w1 = jnp.stack([cmp_w1_k, cmp_w1_v], axis=1).reshape(nl, 2, CMP_LEN * HEAD_DIM, HEAD_DIM)
    bias = _cmp_bias(pe, w1)
    bias2 = jnp.concatenate([bias, bias], axis=-1)
    gates_w = jnp.pad(w_in[:, :, c3 + ATTN_DIM + kv_cols:], ((0, 0), (0, 0), (0, GATE_PAD - 3 * N_HEADS)))
    return {
        "g_mix": g_mix.reshape(nl, 1, d),
        "w_c": w_in[:, :, :c3].astype(BF16),
        "w_q": perm_heads(w_in[:, :, c3:c3 + ATTN_DIM], 2).astype(BF16),
        "w_kv": w_in[:, :, c3 + ATTN_DIM:c3 + ATTN_DIM + kv_cols].astype(BF16),
        "w_g": gates_w.astype(BF16),
        "conv_w": conv_w,
        "g_conv_out": g_conv_out.reshape(nl, 1, CONV_DIM),
        "cmp_w1_k": cmp_w1(cmp_w1_k), "cmp_w1_v": cmp_w1(cmp_w1_v),
        "cmp_b_k": bias2[:, 0], "cmp_b_v": bias2[:, 1],
        "cmp_w2_k": blockdiag2(cmp_w2_k).astype(BF16), "cmp_w2_v": blockdiag2(cmp_w2_v).astype(BF16),
        "g_attn_out": perm_heads(g_attn_out, 1).reshape(nl, 1, ATTN_DIM),
        "w_out_c": w_out[:, :CONV_DIM].astype(BF16),
        "w_out_a": perm_heads(w_out[:, CONV_DIM:], 1).astype(BF16),
        "g_ffn": g_ffn.reshape(nl, 1, d),
        "w_gate": w_gate.astype(BF16), "w_up": w_up.astype(BF16), "w_down": w_down.astype(BF16),
        "g_final": g_final.reshape(1, d),
    }


def kernel(x_prompt, x_sample, cache_kv, cache_win, state_conv, page_table, g_mix, w_in, conv_w, cmp_pe_k, cmp_w1_k, cmp_w2_k, cmp_pe_v, cmp_w1_v, cmp_w2_v, g_conv_out, g_attn_out, w_out, g_ffn, w_gate, w_up, w_down, g_final):
    nl = w_in.shape[0]
    b, s, d = x_prompt.shape
    db, sq = x_sample.shape[:2]
    assert sq == 1 and s % 512 == 0
    n_pages = page_table.shape[1]
    past = n_pages * PAGE_SIZE
    w_len = cache_win.shape[2]
    w_keep = min(WINDOW, s)
    w = _prep_weights(g_mix, w_in, conv_w, cmp_pe_k, cmp_w1_k, cmp_w2_k, cmp_pe_v, cmp_w1_v, cmp_w2_v, g_conv_out,
                      g_attn_out, w_out, g_ffn, w_gate, w_up, w_down, g_final)
    consts = {
        "expand": _gate_expand(),
        "onehot": _sel_onehot(s), "ov": _cmp_to_sel(s // CMP_STRIDE, s // SEL_BLOCK),
        "onehot_s": _sel_onehot(past).T, "ov_s": _cmp_to_sel(past // CMP_STRIDE, (past + 1 + SEL_BLOCK - 1) // SEL_BLOCK),
        "perm": _chunk_perm(),
    }
    cache_kv = cache_kv.transpose(0, 1, 3, 4, 5, 2).reshape(nl, cache_kv.shape[1], N_PAGED * LANES, PAGE_SIZE)
    cache_win = cache_win.transpose(0, 1, 3, 4, 5, 2).reshape(nl, db, 2 * LANES, w_len)

    xp, xs = x_prompt, x_sample
    kvp_l, winp_l, convp_l, kvs_l, wins_l, convs_l = [], [], [], [], [], []
    for l in range(nl):
        final = l == nl - 1
        ycn, q, kvp, kvw, kvb, gates, tail = _premix(xp, l, w)
        kc, vc = _compress(kvp, l, w)
        yan = _prompt_attn(q, kc, vc, kvb, gates, l, w, consts)
        outs = _postmix(xp.reshape(b * s, d), ycn.reshape(b * s, -1), yan.reshape(b * s, -1), l, w, final)
        xp = outs[0].reshape(b, s, d)
        if final:
            y_prompt = outs[1].reshape(b, s, d)
        kvp_l.append(kvp)
        winp_l.append(kvw[:, s - w_keep:])
        convp_l.append(tail[:, 8 - (CONV_WIDTH - 1):])
        ycn, q, kvp, kvw, kvb, gates, u = _premix(xs, l, w, prev=(state_conv[l, :, 1], state_conv[l, :, 0]))
        yan, wout = _sample_attn(q, kvp, kvw, gates, cache_kv, cache_win, page_table, l, w, consts)
        outs = _postmix(xs.reshape(db, d), ycn, yan, l, w, final)
        xs = outs[0].reshape(db, 1, d)
        if final:
            y_sample = outs[1].reshape(db, 1, d)
        kvs_l.append(kvp)
        wins_l.append(wout)
        convs_l.append(jnp.stack([state_conv[l, :, 1], u], axis=1))
    kv_shape = (N_PAGED, N_KV_HEADS, HEAD_DIM)
    return (y_prompt, y_sample,
            jnp.stack(kvp_l).reshape((nl, b, s) + kv_shape),
            jnp.stack(winp_l).reshape((nl, b, w_keep, 2) + kv_shape[1:]),
            jnp.stack(convp_l),
            jnp.stack(kvs_l).reshape((nl, db, 1) + kv_shape),
            jnp.stack(wins_l).reshape((nl, db, 2) + kv_shape[1:] + (w_len,)).transpose(0, 1, 5, 2, 3, 4),
            jnp.stack(convs_l))
```

```python
import functools

import numpy as np
import jax
import jax.numpy as jnp
from jax import lax
from jax.experimental import pallas as pl
from jax.experimental.pallas import tpu as pltpu

HEAD_DIM = 64
N_KV_HEADS = 2
GROUP = 4
N_HEADS = N_KV_HEADS * GROUP
LANES = 2 * HEAD_DIM
CONV_DIM = 512
ATTN_DIM = N_HEADS * HEAD_DIM
CONV_WIDTH = 3
CMP_STRIDE = 16
CMP_LEN = 2 * CMP_STRIDE
SEL_BLOCK = 64
N_SEL = 16
WINDOW = 512
Q_BLOCK = 128
PAGE_SIZE = 128
N_PAGED = 4
RMS_EPS = 1e-6
NEG_INF = -1e30
FORCE_SCORE = 1e4
SCALE = HEAD_DIM ** -0.5
Q_SCALE = SCALE * 1.4426950408889634
GATE_PAD = 128
KEY_CHUNK = 512
VMEM_LIMIT = 56 * 1024 * 1024

F32 = jnp.float32
BF16 = jnp.bfloat16


def _rms(x, g):
    return x * lax.rsqrt(jnp.mean(x * x, axis=-1, keepdims=True) + RMS_EPS) * g


def _dot(a, b):
    return jnp.dot(a, b, preferred_element_type=F32)


def _dot_nt(a, b):
    return lax.dot_general(a, b, (((1,), (1,)), ((), ())), preferred_element_type=F32)


def _halves(dot, a, b):
    h = a.shape[0] // 2
    return jnp.concatenate([dot(a[:h], b), dot(a[h:], b)], axis=0)


def _params(*sem):
    return pltpu.CompilerParams(dimension_semantics=sem, vmem_limit_bytes=VMEM_LIMIT)


def _premix_body(seq_mode, tm, x_ref, g_ref, wc_ref, wq_ref, wkv_ref, wg_ref, cw_ref, gco_ref, *rest):
    if seq_mode:
        ycn_ref, q_ref, kvp_ref, kvw_ref, kvb_ref, gates_ref, tail_ref, ubuf = rest
    else:
        p1_ref, p2_ref, ycn_ref, q_ref, kvp_ref, kvw_ref, kvb_ref, gates_ref, tail_ref = rest
    x = x_ref[...]
    xn = _rms(x, g_ref[...]).astype(BF16)
    pc = _dot(xn, wc_ref[...])
    bg, u = pc[:, :CONV_DIM], pc[:, CONV_DIM:2 * CONV_DIM] * pc[:, 2 * CONV_DIM:]
    if seq_mode:
        @pl.when(pl.program_id(1) == 0)
        def _():
            ubuf[0:8, :] = jnp.zeros((8, CONV_DIM), F32)
        ubuf[8:8 + tm, :] = u
        up1 = ubuf[7:7 + tm, :]
        up2 = ubuf[6:6 + tm, :]
        ubuf[0:8, :] = ubuf[tm:tm + 8, :]
        tail_ref[...] = u[tm - 8:, :]
    else:
        up1, up2 = p1_ref[...], p2_ref[...]
        tail_ref[...] = u
    cw = cw_ref[...]
    yc = bg * (cw[0:1, :] * up2 + cw[1:2, :] * up1 + cw[2:3, :] * u)
    ycn_ref[...] = _rms(yc, gco_ref[...]).astype(BF16)
    q_ref[...] = (_dot(xn, wq_ref[...]) * Q_SCALE).astype(BF16)
    kv = _dot(xn, wkv_ref[...])
    kvp_ref[...] = kv[:, :N_PAGED * LANES]
    kvw_ref[...] = kv[:, N_PAGED * LANES:]
    kvb_ref[...] = kv[:, 2 * LANES:].astype(BF16)
    gates_ref[...] = jax.nn.sigmoid(_dot(xn, wg_ref[...]))


def _premix(x, layer, w, prev=None, tm=512):
    nb, s, d = x.shape
    seq_mode = prev is None
    if seq_mode:
        grid = (nb, s // tm)
        row = lambda c: pl.BlockSpec((None, tm, c), lambda b, i: (b, i, 0))
        tail_spec = pl.BlockSpec((None, 8, CONV_DIM), lambda b, i: (b, 0, 0))
        tail_shape = (nb, 8, CONV_DIM)
        xin = x
        lead = (nb, s)
    else:
        tm = nb
        grid = (1, 1)
        row = lambda c: pl.BlockSpec((tm, c), lambda b, i: (0, 0))
        tail_spec = row(CONV_DIM)
        tail_shape = (nb, CONV_DIM)
        xin = x.reshape(nb, d)
        lead = (nb,)
    wspec = lambda a: pl.BlockSpec((None,) + a.shape[1:], lambda b, i: (layer,) + (0,) * (a.ndim - 1))
    ws = [w["g_mix"], w["w_c"], w["w_q"], w["w_kv"], w["w_g"], w["conv_w"], w["g_conv_out"]]
    in_specs = [row(d)] + [wspec(a) for a in ws]
    args = [xin] + ws
    scratch = []
    if seq_mode:
        scratch = [pltpu.VMEM((tm + 8, CONV_DIM), F32)]
    else:
        in_specs += [row(CONV_DIM), row(CONV_DIM)]
        args += list(prev)
    out_shape = [
        jax.ShapeDtypeStruct(lead + (CONV_DIM,), BF16),
        jax.ShapeDtypeStruct(lead + (ATTN_DIM,), BF16),
        jax.ShapeDtypeStruct(lead + (N_PAGED * LANES,), F32),
        jax.ShapeDtypeStruct(lead + (2 * LANES,), F32),
        jax.ShapeDtypeStruct(lead + (4 * LANES,), BF16),
        jax.ShapeDtypeStruct(lead + (GATE_PAD,), F32),
        jax.ShapeDtypeStruct(tail_shape, F32),
    ]
    out_specs = [row(CONV_DIM), row(ATTN_DIM), row(N_PAGED * LANES), row(2 * LANES), row(4 * LANES),
                 row(GATE_PAD), tail_spec]
    return pl.pallas_call(
        functools.partial(_premix_body, seq_mode, tm),
        grid=grid, in_specs=in_specs, out_specs=out_specs, out_shape=out_shape,
        scratch_shapes=scratch, compiler_params=_params("arbitrary", "arbitrary"),
        name="premix_seq" if seq_mode else "premix_step",
    )(*args)


def _cmp_bias_body(pe_ref, w1_ref, o_ref):
    o_ref[...] = jnp.sum(pe_ref[...] * w1_ref[...], axis=0, keepdims=True)


def _cmp_bias(pe, w1):
    nl = pe.shape[0]
    k = CMP_LEN * HEAD_DIM
    return pl.pallas_call(
        _cmp_bias_body,
        grid=(nl, 2),
        in_specs=[pl.BlockSpec((None, None, k, 1), lambda l, t: (l, t, 0, 0)),
                  pl.BlockSpec((None, None, k, HEAD_DIM), lambda l, t: (l, t, 0, 0))],
        out_specs=pl.BlockSpec((None, None, 1, HEAD_DIM), lambda l, t: (l, t, 0, 0)),
        out_shape=jax.ShapeDtypeStruct((nl, 2, 1, HEAD_DIM), F32),
        compiler_params=_params("arbitrary", "arbitrary"), name="cmp_bias",
    )(pe, w1)


def _cmp_finish(acc, bias, w2, sbuf):
    n = acc.shape[0]
    sbuf[0:n, :] = acc[:, LANES:]
    sbuf[n:n + 8, :] = jnp.zeros((8, LANES), F32)
    h = jax.nn.gelu(acc[:, :LANES] + sbuf[1:n + 1, :] + bias)
    return _dot(h.astype(BF16), w2)


def _compress_body(nch, k_ref, v_ref, wk_ref, wv_ref, bk_ref, bv_ref, w2k_ref, w2v_ref, kc_ref, vc_ref, sbuf):
    for src, w1, b, w2, out in ((k_ref, wk_ref, bk_ref, w2k_ref, kc_ref), (v_ref, wv_ref, bv_ref, w2v_ref, vc_ref)):
        acc = jnp.zeros((nch, 2 * LANES), F32)
        for t in range(CMP_STRIDE // 2):
            lhs = jnp.concatenate([src[pl.ds(j, nch, stride=CMP_STRIDE), :] for j in (2 * t, 2 * t + 1)], axis=1)
            acc = acc + _dot(lhs.astype(BF16), w1[t])
        out[...] = _cmp_finish(acc, b[...], w2[...], sbuf).astype(BF16)


def _compress(kvp, layer, w):
    nb, s, _ = kvp.shape
    nch = s // CMP_STRIDE
    wspec = lambda a: pl.BlockSpec((None,) + a.shape[1:], lambda b: (layer,) + (0,) * (a.ndim - 1))
    ws = [w["cmp_w1_k"], w["cmp_w1_v"], w["cmp_b_k"], w["cmp_b_v"], w["cmp_w2_k"], w["cmp_w2_v"]]
    return pl.pallas_call(
        functools.partial(_compress_body, nch),
        grid=(nb,),
        in_specs=[pl.BlockSpec((None, s, LANES), lambda b: (b, 0, 0)),
                  pl.BlockSpec((None, s, LANES), lambda b: (b, 0, 1))] + [wspec(a) for a in ws],
        out_specs=[pl.BlockSpec((None, nch, LANES), lambda b: (b, 0, 0))] * 2,
        out_shape=[jax.ShapeDtypeStruct((nb, nch, LANES), BF16)] * 2,
        scratch_shapes=[pltpu.VMEM((nch + 8, LANES), F32)],
        compiler_params=_params("arbitrary"), name="compress",
    )(kvp, kvp, *ws)


def _half_mask(rows):
    return lax.broadcasted_iota(jnp.int32, (rows, LANES), 1) < HEAD_DIM


def _stack_q(qt):
    n = qt.shape[0]
    lo = _half_mask(n)
    zero = jnp.zeros((n, LANES), qt.dtype)
    parts = []
    for kvh in range(N_KV_HEADS):
        for g in range(GROUP):
            blk = qt[:, g * LANES:(g + 1) * LANES]
            parts.append(jnp.where(lo if kvh == 0 else ~lo, blk, zero))
    return jnp.concatenate(parts, axis=0)


def _unstack_o(o, n):
    lo = _half_mask(n)
    cols = [jnp.where(lo, o[g * n:(g + 1) * n, :], o[(GROUP + g) * n:(GROUP + g + 1) * n, :]) for g in range(GROUP)]
    return jnp.concatenate(cols, axis=1)


def _masked_softmax(s, mask):
    m = jnp.max(jnp.where(mask, s, NEG_INF), axis=-1, keepdims=True)
    e = jnp.where(mask, jnp.exp2(s - m), 0.0)
    l = jnp.sum(e, axis=-1, keepdims=True)
    return e / jnp.where(l > 0.0, l, 1.0)


def _add_tile_bias(s, bias):
    n = bias.shape[0]
    return jnp.concatenate([s[r * n:(r + 1) * n, :] + bias for r in range(s.shape[0] // n)], axis=0)


def _split_bf16(x, terms):
    out = []
    for _ in range(terms):
        hi = x.astype(BF16)
        out.append(hi)
        x = x - hi.astype(F32)
    return out


def _dot_exact_rhs(x, rhs_bf16, terms):
    return sum(_dot(t, rhs_bf16) for t in _split_bf16(x, terms))


def _combine(gates, expand, o_c, o_s, o_w, g_out):
    gx = _dot_exact_rhs(gates, expand, 2)
    ya = gx[:, :ATTN_DIM] * o_c + gx[:, ATTN_DIM:2 * ATTN_DIM] * o_s + gx[:, 2 * ATTN_DIM:] * o_w
    return _rms(ya, g_out).astype(BF16)


def _cmp_to_sel(nc_rows, ns):
    n = np.arange(nc_rows)[:, None]
    s = np.arange(LANES)[None, :]
    ov = (n * CMP_STRIDE <= s * SEL_BLOCK + SEL_BLOCK - 1) & (n * CMP_STRIDE + CMP_LEN - 1 >= s * SEL_BLOCK) & (s < ns)
    return jnp.asarray(ov.astype(np.float32), dtype=BF16)


def _gate_expand():
    e = np.zeros((GATE_PAD, 3 * ATTN_DIM), np.float32)
    for br in range(3):
        for kvh in range(N_KV_HEADS):
            for g in range(GROUP):
                c0 = br * ATTN_DIM + g * LANES + kvh * HEAD_DIM
                e[br * N_HEADS + kvh * GROUP + g, c0:c0 + HEAD_DIM] = 1.0
    return jnp.asarray(e, dtype=BF16)


def _sel_onehot(n_keys):
    t = np.arange(n_keys)[:, None] // SEL_BLOCK
    return jnp.asarray((t == np.arange(LANES)[None, :]).astype(np.float32), dtype=BF16)


def _chunk_perm():
    m = np.arange(PAGE_SIZE)[:, None]
    t = np.arange(PAGE_SIZE)[None, :]
    per_page = PAGE_SIZE // CMP_STRIDE
    return jnp.asarray((t == CMP_STRIDE * (m % per_page) + m // per_page).astype(np.float32), dtype=BF16)


def _rank_penalty(imp_t, ns):
    r = imp_t.shape[1]
    sub = lax.broadcasted_iota(jnp.int32, (8, r), 0)
    cnt = jnp.zeros((ns, r), F32)
    for sp in range(ns):
        other = imp_t[sp:sp + 1, :]
        parts = []
        for blk in range(ns // 8):
            t = imp_t[blk * 8:(blk + 1) * 8, :]
            if blk * 8 > sp:
                parts.append(jnp.where(other >= t, 1.0, 0.0))
            elif blk * 8 + 7 <= sp:
                parts.append(jnp.where(other > t, 1.0, 0.0))
            else:
                parts.append(jnp.where(sub > sp - blk * 8, jnp.where(other >= t, 1.0, 0.0), jnp.where(other > t, 1.0, 0.0)))
        cnt = cnt + jnp.concatenate(parts, axis=0)
    return jnp.where(cnt < N_SEL, 0.0, NEG_INF)


def _prompt_attn_body(ns, q_ref, kc_ref, vc_ref, ksl_ref, vsl_ref, kw_ref, vw_ref, oh_ref, ov_ref, gates_ref,
                      ex_ref, go_ref, out_ref, pen_s):
    i = pl.program_id(1)
    nq = Q_BLOCK
    rows = N_HEADS * nq
    qz = _stack_q(q_ref[...])
    qpos_t = i * nq + lax.broadcasted_iota(jnp.int32, (nq, 1), 0)

    kc = kc_ref[...]
    nc = kc.shape[0]
    c_end = lax.broadcasted_iota(jnp.int32, (nq, nc), 1) * CMP_STRIDE + (CMP_LEN - 1)
    s_c = _add_tile_bias(_halves(_dot_nt, qz, kc), jnp.where(c_end <= qpos_t, 0.0, NEG_INF))
    e_c = jnp.exp2(s_c - jnp.max(s_c, axis=-1, keepdims=True))
    any_c = jnp.where(qpos_t >= CMP_LEN - 1, 1.0, 0.0)
    p_c = e_c * (jnp.concatenate([any_c] * N_HEADS, axis=0) / jnp.sum(e_c, axis=-1, keepdims=True))
    o_c = _dot(p_c.astype(BF16), vc_ref[...])

    psum = jnp.concatenate(
        [sum(p_c[(kvh * GROUP + g) * nq:(kvh * GROUP + g + 1) * nq, :] for g in range(GROUP))
         for kvh in range(N_KV_HEADS)], axis=0)
    imp = _dot_exact_rhs(psum, ov_ref[...], 3)
    r2 = N_KV_HEADS * nq
    s_idx = lax.broadcasted_iota(jnp.int32, (r2, LANES), 1)
    q_blk = (i * nq + lax.broadcasted_iota(jnp.int32, (r2, LANES), 0) % nq) // SEL_BLOCK
    forced = (s_idx == 0) | (s_idx == q_blk) | (s_idx == q_blk - 1)
    imp = jnp.where(s_idx <= q_blk, jnp.where(forced, FORCE_SCORE, imp), -1.0)
    imp = jnp.where(s_idx < ns, imp, -2.0)
    pen_s[...] = jnp.zeros((LANES, r2), F32)

    @pl.when((i + 1) * nq > N_SEL * SEL_BLOCK)
    def _():
        pen_s[0:ns, :] = _rank_penalty(imp.T[0:ns, :], ns)

    pen = pen_s[...].T
    pen = jnp.where(s_idx <= q_blk, pen, jnp.where(s_idx < ns, NEG_INF, 0.0)).astype(BF16)
    pen_rows = jnp.concatenate([pen[kvh * nq:(kvh + 1) * nq, :] for kvh in range(N_KV_HEADS) for _ in range(GROUP)],
                               axis=0)
    q_aug = jnp.concatenate([qz, pen_rows], axis=1)

    ones_v = jnp.ones((KEY_CHUNK, LANES), BF16)

    def scores(c):
        k0 = pl.multiple_of(c * KEY_CHUNK, KEY_CHUNK)
        k_aug = jnp.concatenate([ksl_ref[pl.ds(k0, KEY_CHUNK), :], oh_ref[pl.ds(k0, KEY_CHUNK), :]], axis=1)
        return _halves(_dot_nt, q_aug, k_aug).astype(BF16)

    def update(c, s, m_old, acc):
        k0 = pl.multiple_of(c * KEY_CHUNK, KEY_CHUNK)
        m_new = jnp.maximum(m_old, jnp.max(s, axis=-1, keepdims=True).astype(F32))
        p = jnp.exp2(s - m_new.astype(BF16))
        v_aug = jnp.concatenate([vsl_ref[pl.ds(k0, KEY_CHUNK), :], ones_v], axis=1)
        return m_new, jnp.exp2(m_old - m_new) * acc + _dot(p, v_aug)

    last = (i * nq) // KEY_CHUNK
    carry = (jnp.full((rows, 1), NEG_INF, F32), jnp.zeros((rows, 2 * LANES), F32))
    m_old, acc = lax.fori_loop(0, last, lambda c, cr: update(c, scores(c), *cr), carry)
    kpos = last * KEY_CHUNK + lax.broadcasted_iota(jnp.int32, (nq, KEY_CHUNK), 1)
    _, acc = update(last, _add_tile_bias(scores(last), jnp.where(kpos <= qpos_t, 0.0, NEG_INF).astype(BF16)), m_old, acc)
    o_s = acc[:, :LANES] / acc[:, LANES:]

    nw = WINDOW + nq
    w0 = pl.multiple_of(jnp.maximum(i * nq - WINDOW, 0), nq)
    diff = qpos_t - (w0 + lax.broadcasted_iota(jnp.int32, (nq, nw), 1))
    s_w = _add_tile_bias(_halves(_dot_nt, qz, kw_ref[pl.ds(w0, nw), :]).astype(BF16),
                         jnp.where((diff >= 0) & (diff < WINDOW), 0.0, NEG_INF).astype(BF16))
    e_w = jnp.exp2(s_w - jnp.max(s_w, axis=-1, keepdims=True))
    ow_aug = _dot(e_w, jnp.concatenate([vw_ref[pl.ds(w0, nw), :], jnp.ones((nw, LANES), BF16)], axis=1))
    o_w = ow_aug[:, :LANES] / ow_aug[:, LANES:]

    out_ref[...] = _combine(gates_ref[...], ex_ref[...], _unstack_o(o_c, nq), _unstack_o(o_s, nq),
                            _unstack_o(o_w, nq), go_ref[...])


def _prompt_attn(q, kc, vc, kvb, gates, layer, w, consts):
    nb, s, _ = q.shape
    nc = kc.shape[1]
    ns = s // SEL_BLOCK
    col = lambda c: pl.BlockSpec((None, s, LANES), lambda b, i: (b, 0, c))
    whole = lambda a: pl.BlockSpec(a.shape, lambda b, i: (0,) * a.ndim)
    return pl.pallas_call(
        functools.partial(_prompt_attn_body, ns),
        grid=(nb, s // Q_BLOCK),
        in_specs=[pl.BlockSpec((None, Q_BLOCK, ATTN_DIM), lambda b, i: (b, i, 0)),
                  pl.BlockSpec((None, nc, LANES), lambda b, i: (b, 0, 0)),
                  pl.BlockSpec((None, nc, LANES), lambda b, i: (b, 0, 0)),
                  col(0), col(1), col(2), col(3),
                  whole(consts["onehot"]), whole(consts["ov"]),
                  pl.BlockSpec((None, Q_BLOCK, GATE_PAD), lambda b, i: (b, i, 0)),
                  whole(consts["expand"]),
                  pl.BlockSpec((None, 1, ATTN_DIM), lambda b, i: (layer, 0, 0))],
        out_specs=pl.BlockSpec((None, Q_BLOCK, ATTN_DIM), lambda b, i: (b, i, 0)),
        out_shape=jax.ShapeDtypeStruct((nb, s, ATTN_DIM), BF16),
        scratch_shapes=[pltpu.VMEM((LANES, N_KV_HEADS * Q_BLOCK), F32)],
        compiler_params=_params("arbitrary", "arbitrary"), name="prompt_attn",
    )(q, kc, vc, kvb, kvb, kvb, kvb, consts["onehot"], consts["ov"], gates, consts["expand"], w["g_attn_out"])


def _sample_attn_body(n_pages, nseq, pt_ref, q_ref, kvp_ref, kvw_ref, gates_ref, *rest):
    n_pg = nseq * n_pages
    pages = rest[:n_pg]
    (win_ref, wk_ref, wv_ref, bk_ref, bv_ref, w2k_ref, w2v_ref, perm_ref, oh_ref, ov_ref, ex_ref,
     go_ref) = rest[n_pg:n_pg + 12]
    out_ref, wout_ref, sbuf = rest[-3:]
    past = n_pages * PAGE_SIZE
    qpos = past
    nch = past // CMP_STRIDE
    per_page = PAGE_SIZE // CMP_STRIDE
    seqs = range(nseq)
    R = nseq * N_HEADS
    perm = perm_ref[...]

    xs = [_dot_nt(perm, pg[0:2 * LANES, :].astype(BF16)) for pg in pages]

    def compress(c0, w1, b, w2):
        acc = jnp.zeros((nseq * nch, 2 * LANES), F32)
        for t in range(CMP_STRIDE // 2):
            lhs = jnp.concatenate(
                [jnp.concatenate([x[j * per_page:(j + 1) * per_page, c0:c0 + LANES] for j in (2 * t, 2 * t + 1)], axis=1)
                 for x in xs], axis=0)
            acc = acc + _dot(lhs.astype(BF16), w1[t])
        c = _cmp_finish(acc, b[...], w2[...], sbuf).astype(BF16)
        return [c[sq * nch:(sq + 1) * nch, :] for sq in seqs]

    kcs = compress(0, wk_ref, bk_ref, w2k_ref)
    vcs = compress(LANES, wv_ref, bv_ref, w2v_ref)

    ns = (past + 1 + SEL_BLOCK - 1) // SEL_BLOCK
    q_blk = qpos // SEL_BLOCK
    w_len = win_ref.shape[2]
    lane = lax.broadcasted_iota(jnp.int32, (R, LANES), 1)
    rows_of = lambda x, sq: x[sq * N_HEADS:(sq + 1) * N_HEADS, :]
    per_row = lambda x: jnp.concatenate([jnp.broadcast_to(x[sq:sq + 1, :], (N_HEADS, x.shape[1])) for sq in seqs], axis=0)
    as_bf16 = lambda x: x.astype(BF16).astype(F32)
    qzs = [_stack_q(q_ref[sq:sq + 1, :]) for sq in seqs]
    qz = jnp.concatenate(qzs, axis=0)
    qf = qz.astype(F32)
    kv_new = per_row(kvp_ref[...])
    kvw_new = per_row(kvw_ref[...])

    s_c = jnp.concatenate([_dot_nt(qzs[sq], kcs[sq]) for sq in seqs], axis=0)
    blk = lax.broadcasted_iota(jnp.int32, (R, nch), 1)
    p_c = _masked_softmax(s_c, (blk * CMP_STRIDE + (CMP_LEN - 1) <= qpos) & (blk < nch - 1))
    p_cb = p_c.astype(BF16)
    o_c = jnp.concatenate([_dot(rows_of(p_cb, sq), vcs[sq]) for sq in seqs], axis=0)

    psum = jnp.concatenate(
        [jnp.sum(p_c[sq * N_HEADS + kvh * GROUP:sq * N_HEADS + (kvh + 1) * GROUP, :], axis=0, keepdims=True)
         if kvh < N_KV_HEADS else jnp.zeros((N_HEADS - N_KV_HEADS, nch), F32)
         for sq in seqs for kvh in range(N_KV_HEADS + 1)], axis=0)
    imp = _dot_exact_rhs(psum, ov_ref[...], 3)
    forced = (lane == 0) | (lane == q_blk) | (lane == q_blk - 1)
    imp = jnp.where(lane <= q_blk, jnp.where(forced, FORCE_SCORE, imp), -1.0)
    imp = jnp.where(lane < ns, imp, -2.0)
    imp_t = imp.T
    sub = lax.broadcasted_iota(jnp.int32, (LANES, LANES), 0)
    lan = lax.broadcasted_iota(jnp.int32, (LANES, LANES), 1)
    pens = []
    for sq in seqs:
        for kvh in range(N_KV_HEADS):
            r = sq * N_HEADS + kvh
            other = imp_t[:, r:r + 1]
            mine = imp[r:r + 1, :]
            beats = (other > mine) | ((other == mine) & (sub < lan))
            cnt = jnp.sum(beats.astype(jnp.int32), axis=0, keepdims=True)
            sel = (cnt < N_SEL) & (lane[0:1, :] <= q_blk)
            pens += [jnp.where(sel | (lane[0:1, :] >= ns), 0.0, NEG_INF)] * GROUP
    pen = jnp.concatenate(pens, axis=0)
    q_aug = jnp.concatenate([qz, pen.astype(BF16)], axis=1)

    s_s = jnp.concatenate(
        [jnp.concatenate(
            [_dot(rows_of(q_aug, sq), jnp.concatenate([pg[2 * LANES:3 * LANES, :].astype(BF16),
                                                       oh_ref[:, p * PAGE_SIZE:(p + 1) * PAGE_SIZE]], axis=0))
             for p, pg in enumerate(pages[sq * n_pages:(sq + 1) * n_pages])], axis=1)
         for sq in seqs], axis=0)
    pen_new = jnp.sum(jnp.where(lane == q_blk, pen, 0.0), axis=-1, keepdims=True)
    s_new = jnp.sum(qf * as_bf16(kv_new[:, 2 * LANES:3 * LANES]), axis=-1, keepdims=True) + pen_new
    m = jnp.maximum(jnp.max(s_s, axis=-1, keepdims=True), s_new)
    p_s = jnp.exp2(s_s - m)
    p_new = jnp.exp2(s_new - m)
    den = jnp.sum(p_s, axis=-1, keepdims=True) + p_new
    p_sb = p_s.astype(BF16)
    pv = []
    for sq in seqs:
        o = jnp.zeros((N_HEADS, LANES), F32)
        for p, pg in enumerate(pages[sq * n_pages:(sq + 1) * n_pages]):
            o = o + _dot_nt(rows_of(p_sb, sq)[:, p * PAGE_SIZE:(p + 1) * PAGE_SIZE], pg[3 * LANES:4 * LANES, :].astype(BF16))
        pv.append(o)
    o_s = (jnp.concatenate(pv, axis=0) + as_bf16(p_new) * as_bf16(kv_new[:, 3 * LANES:4 * LANES])) / den

    wins = [win_ref[sq] for sq in seqs]
    s_w = jnp.concatenate([_dot(qzs[sq], wins[sq][:LANES, :].astype(BF16)) for sq in seqs], axis=0)
    diff = qpos - (past - w_len + lax.broadcasted_iota(jnp.int32, (R, w_len), 1))
    ok = (diff >= 0) & (diff < WINDOW)
    s_wn = jnp.sum(qf * as_bf16(kvw_new[:, :LANES]), axis=-1, keepdims=True)
    m = jnp.maximum(jnp.max(jnp.where(ok, s_w, NEG_INF), axis=-1, keepdims=True), s_wn)
    p_w = jnp.where(ok, jnp.exp2(s_w - m), 0.0)
    p_wn = jnp.exp2(s_wn - m)
    den = jnp.sum(p_w, axis=-1, keepdims=True) + p_wn
    p_wb = p_w.astype(BF16)
    o_w = (jnp.concatenate([_dot_nt(rows_of(p_wb, sq), wins[sq][LANES:, :].astype(BF16)) for sq in seqs], axis=0)
           + as_bf16(p_wn) * as_bf16(kvw_new[:, LANES:])) / den

    o_c, o_s, o_w = (jnp.concatenate([_unstack_o(rows_of(o, sq), 1) for sq in seqs], axis=0) for o in (o_c, o_s, o_w))
    out_ref[...] = _combine(gates_ref[...], ex_ref[...], o_c, o_s, o_w, go_ref[...])

    eye_w = (lax.broadcasted_iota(jnp.int32, (2 * LANES, 2 * LANES), 0)
             == lax.broadcasted_iota(jnp.int32, (2 * LANES, 2 * LANES), 1))
    last_lane = lax.broadcasted_iota(jnp.int32, (2 * LANES, w_len), 1) == w_len - 1
    for sq in seqs:
        new_col = jnp.sum(jnp.where(eye_w, kvw_ref[sq:sq + 1, :], 0.0), axis=1, keepdims=True)
        wout_ref[sq] = jnp.where(last_lane, new_col, pltpu.roll(wins[sq], w_len - 1, 1))


def _sample_attn(q, kvp, kvw, gates, cache_kv_t, cache_win_t, page_table, layer, w, consts, win_all=None, nseq=4):
    db, n_pages = page_table.shape
    nl, w_len = cache_win_t.shape[0], cache_win_t.shape[3]
    nch = n_pages * PAGE_SIZE // CMP_STRIDE
    assert db % nseq == 0
    r3 = lambda a: a.reshape(db // nseq, nseq, a.shape[-1])
    rowspec = lambda c: pl.BlockSpec((None, nseq, c), lambda b, pt: (b, 0, 0))
    whole = lambda a: pl.BlockSpec(a.shape, lambda b, pt: (0,) * a.ndim)
    wspec = lambda a: pl.BlockSpec((None,) + a.shape[1:], lambda b, pt: (layer,) + (0,) * (a.ndim - 1))
    page_specs = [pl.BlockSpec((None, None, N_PAGED * LANES, PAGE_SIZE),
                               functools.partial(lambda sq, p, b, pt: (layer, pt[b * nseq + sq, p], 0, 0), sq, p))
                  for sq in range(nseq) for p in range(n_pages)]
    ws = [w["cmp_w1_k"], w["cmp_w1_v"], w["cmp_b_k"], w["cmp_b_v"], w["cmp_w2_k"], w["cmp_w2_v"]]
    cs = [consts["perm"], consts["onehot_s"], consts["ov_s"], consts["expand"]]
    win_spec = pl.BlockSpec((None, nseq, 2 * LANES, w_len), lambda b, pt: (layer, b, 0, 0))
    args = [page_table, r3(q), r3(kvp), r3(kvw), r3(gates), *([cache_kv_t] * (nseq * n_pages)), cache_win_t, *ws, *cs,
            w["g_attn_out"]]
    in_specs = ([rowspec(ATTN_DIM), rowspec(N_PAGED * LANES), rowspec(2 * LANES), rowspec(GATE_PAD)] + page_specs
                + [win_spec] + [wspec(a) for a in ws] + [whole(a) for a in cs] + [wspec(w["g_attn_out"])])
    aliases = {}
    if win_all is not None:
        in_specs.append(pl.BlockSpec(memory_space=pl.ANY))
        args.append(win_all)
        aliases = {len(args) - 1: 1}
    grid_spec = pltpu.PrefetchScalarGridSpec(
        num_scalar_prefetch=1, grid=(db // nseq,), in_specs=in_specs, out_specs=[rowspec(ATTN_DIM), win_spec],
        scratch_shapes=[pltpu.VMEM((nseq * nch + 8, LANES), F32)],
    )
    out, win_all = pl.pallas_call(
        functools.partial(_sample_attn_body, n_pages, nseq),
        grid_spec=grid_spec,
        out_shape=[jax.ShapeDtypeStruct((db // nseq, nseq, ATTN_DIM), BF16),
                   jax.ShapeDtypeStruct((nl, db, 2 * LANES, w_len), F32)],
        input_output_aliases=aliases,
        compiler_params=_params("arbitrary"), name="sample_attn",
    )(*args)
    return out.reshape(db, ATTN_DIM), win_all


def _postmix_body(final, x_ref, yc_ref, ya_ref, woc_ref, woa_ref, gf_ref, wg_ref, wu_ref, wd_ref, gl_ref, xo_ref, *rest):
    h_s, acc_s = rest[-2:]
    f = pl.program_id(1)

    @pl.when(f == 0)
    def _():
        x1 = x_ref[...] + _dot(yc_ref[...], woc_ref[...]) + _dot(ya_ref[...], woa_ref[...])
        h_s[...] = _rms(x1, gf_ref[...]).astype(BF16)
        acc_s[...] = x1

    h = h_s[...]
    a = _dot(h, wg_ref[...])
    act = (a * jax.nn.sigmoid(a) * _dot(h, wu_ref[...])).astype(BF16)
    acc_s[...] += _dot(act, wd_ref[...])

    @pl.when(f == pl.num_programs(1) - 1)
    def _():
        out = acc_s[...]
        xo_ref[...] = out
        if final:
            rest[0][...] = _rms(out, gl_ref[...])


def _postmix(x, ycn, yan, layer, w, final, tm=512, n_ff=2):
    t, d = x.shape
    tm = min(tm, t)
    d_ff = w["w_gate"].shape[-1]
    tf = d_ff // n_ff
    assert tf * n_ff == d_ff and tf % LANES == 0
    row = lambda c: pl.BlockSpec((tm, c), lambda i, f: (i, 0))
    wspec = lambda a: pl.BlockSpec((None,) + a.shape[1:], lambda i, f: (layer,) + (0,) * (a.ndim - 1))
    gl = w["g_final"]
    n_out = 2 if final else 1
    outs = pl.pallas_call(
        functools.partial(_postmix_body, final),
        grid=(t // tm, n_ff),
        in_specs=[row(d), row(CONV_DIM), row(ATTN_DIM), wspec(w["w_out_c"]), wspec(w["w_out_a"]), wspec(w["g_ffn"]),
                  pl.BlockSpec((None, d, tf), lambda i, f: (layer, 0, f)),
                  pl.BlockSpec((None, d, tf), lambda i, f: (layer, 0, f)),
                  pl.BlockSpec((None, tf, d), lambda i, f: (layer, f, 0)),
                  pl.BlockSpec(gl.shape, lambda i, f: (0, 0))],
        out_specs=[row(d)] * n_out,
        out_shape=[jax.ShapeDtypeStruct((t, d), F32)] * n_out,
        scratch_shapes=[pltpu.VMEM((tm, d), BF16), pltpu.VMEM((tm, d), F32)],
        compiler_params=_params("arbitrary", "arbitrary"), name="postmix",
    )(x, ycn, yan, w["w_out_c"], w["w_out_a"], w["g_ffn"], w["w_gate"], w["w_up"], w["w_down"], gl)
    return outs


def _prep_weights(g_mix, w_in, conv_w, cmp_pe_k, cmp_w1_k, cmp_w2_k, cmp_pe_v, cmp_w1_v, cmp_w2_v, g_conv_out,
                  g_attn_out, w_out, g_ffn, w_gate, w_up, w_down, g_final):
    nl, d, _ = w_in.shape
    c3 = 3 * CONV_DIM
    kv_cols = 6 * LANES

    def perm_heads(a, axis):
        shp = a.shape
        a = a.reshape(shp[:axis] + (N_KV_HEADS, GROUP, HEAD_DIM) + shp[axis + 1:])
        a = jnp.swapaxes(a, axis, axis + 1)
        return a.reshape(shp)

    def blockdiag2(a):
        z = jnp.zeros_like(a)
        return jnp.concatenate([jnp.concatenate([a, z], axis=-1), jnp.concatenate([z, a], axis=-1)], axis=-2)

    def cmp_w1(w1):
        a = blockdiag2(w1[:, :CMP_STRIDE])
        b = blockdiag2(w1[:, CMP_STRIDE:])
        return jnp.concatenate([a, b], axis=-1).astype(BF16).reshape(nl, CMP_STRIDE // 2, 2 * LANES, 2 * LANES)

    pe = jnp.stack([cmp_pe_k, cmp_pe_v], axis=1).reshape(nl, 2, CMP_LEN * HEAD_DIM, 1)
    w1 = jnp.stack([cmp_w1_k, cmp_w1_v], axis=1).reshape(nl, 2, CMP_LEN * HEAD_DIM, HEAD_DIM)
    bias = _cmp_bias(pe, w1)
    bias2 = jnp.concatenate([bias, bias], axis=-1)
    gates_w = jnp.pad(w_in[:, :, c3 + ATTN_DIM + kv_cols:], ((0, 0), (0, 0), (0, GATE_PAD - 3 * N_HEADS)))
    return {
        "g_mix": g_mix.reshape(nl, 1, d),
        "w_c": w_in[:, :, :c3].astype(BF16),
        "w_q": perm_heads(w_in[:, :, c3:c3 + ATTN_DIM], 2).astype(BF16),
        "w_kv": w_in[:, :, c3 + ATTN_DIM:c3 + ATTN_DIM + kv_cols].astype(BF16),
        "w_g": gates_w.astype(BF16),
        "conv_w": conv_w,
        "g_conv_out": g_conv_out.reshape(nl, 1, CONV_DIM),
        "cmp_w1_k": cmp_w1(cmp_w1_k), "cmp_w1_v": cmp_w1(cmp_w1_v),
        "cmp_b_k": bias2[:, 0], "cmp_b_v": bias2[:, 1],
        "cmp_w2_k": blockdiag2(cmp_w2_k).astype(BF16), "cmp_w2_v": blockdiag2(cmp_w2_v).astype(BF16),
        "g_attn_out": perm_heads(g_attn_out, 1).reshape(nl, 1, ATTN_DIM),
        "w_out_c": w_out[:, :CONV_DIM].astype(BF16),
        "w_out_a": perm_heads(w_out[:, CONV_DIM:], 1).astype(BF16),
        "g_ffn": g_ffn.reshape(nl, 1, d),
        "w_gate": w_gate.astype(BF16), "w_up": w_up.astype(BF16), "w_down": w_down.astype(BF16),
        "g_final": g_final.reshape(1, d),
    }


def kernel(x_prompt, x_sample, cache_kv, cache_win, state_conv, page_table, g_mix, w_in, conv_w, cmp_pe_k, cmp_w1_k, cmp_w2_k, cmp_pe_v, cmp_w1_v, cmp_w2_v, g_conv_out, g_attn_out, w_out, g_ffn, w_gate, w_up, w_down, g_final):
    nl = w_in.shape[0]
    b, s, d = x_prompt.shape
    db, sq = x_sample.shape[:2]
    assert sq == 1 and s % 512 == 0
    n_pages = page_table.shape[1]
    past = n_pages * PAGE_SIZE
    w_len = cache_win.shape[2]
    w_keep = min(WINDOW, s)
    w = _prep_weights(g_mix, w_in, conv_w, cmp_pe_k, cmp_w1_k, cmp_w2_k, cmp_pe_v, cmp_w1_v, cmp_w2_v, g_conv_out,
                      g_attn_out, w_out, g_ffn, w_gate, w_up, w_down, g_final)
    consts = {
        "expand": _gate_expand(),
        "onehot": _sel_onehot(s), "ov": _cmp_to_sel(s // CMP_STRIDE, s // SEL_BLOCK),
        "onehot_s": _sel_onehot(past).T, "ov_s": _cmp_to_sel(past // CMP_STRIDE, (past + 1 + SEL_BLOCK - 1) // SEL_BLOCK),
        "perm": _chunk_perm(),
    }
    cache_kv = cache_kv.transpose(0, 1, 3, 4, 5, 2).reshape(nl, cache_kv.shape[1], N_PAGED * LANES, PAGE_SIZE)
    cache_win = cache_win.transpose(0, 1, 3, 4, 5, 2).reshape(nl, db, 2 * LANES, w_len)

    xp, xs = x_prompt, x_sample
    kvp_l, winp_l, convp_l, kvs_l, convs_l = [], [], [], [], []
    win_all = None
    for l in range(nl):
        final = l == nl - 1
        ycn, q, kvp, kvw, kvb, gates, tail = _premix(xp, l, w)
        kc, vc = _compress(kvp, l, w)
        yan = _prompt_attn(q, kc, vc, kvb, gates, l, w, consts)
        outs = _postmix(xp.reshape(b * s, d), ycn.reshape(b * s, -1), yan.reshape(b * s, -1), l, w, final)
        xp = outs[0].reshape(b, s, d)
        if final:
            y_prompt = outs[1].reshape(b, s, d)
        kvp_l.append(kvp)
        winp_l.append(kvw[:, s - w_keep:])
        convp_l.append(tail[:, 8 - (CONV_WIDTH - 1):])
        ycn, q, kvp, kvw, kvb, gates, u = _premix(xs, l, w, prev=(state_conv[l, :, 1], state_conv[l, :, 0]))
        yan, win_all = _sample_attn(q, kvp, kvw, gates, cache_kv, cache_win, page_table, l, w, consts, win_all)
        outs = _postmix(xs.reshape(db, d), ycn, yan, l, w, final)
        xs = outs[0].reshape(db, 1, d)
        if final:
            y_sample = outs[1].reshape(db, 1, d)
        kvs_l.append(kvp)
        convs_l.append(jnp.stack([state_conv[l, :, 1], u], axis=1))
    kv_shape = (N_PAGED, N_KV_HEADS, HEAD_DIM)
    return (y_prompt, y_sample,
            jnp.stack(kvp_l).reshape((nl, b, s) + kv_shape),
            jnp.stack(winp_l).reshape((nl, b, w_keep, 2) + kv_shape[1:]),
            jnp.stack(convp_l),
            jnp.stack(kvs_l).reshape((nl, db, 1) + kv_shape),
            win_all.reshape((nl, db, 2) + kv_shape[1:] + (w_len,)).transpose(0, 1, 5, 2, 3, 4),
            jnp.stack(convs_l))
```

```python
import functools

import numpy as np
import jax
import jax.numpy as jnp
from jax import lax
from jax.experimental import pallas as pl
from jax.experimental.pallas import tpu as pltpu

HEAD_DIM = 64
N_KV_HEADS = 2
GROUP = 4
N_HEADS = N_KV_HEADS * GROUP
LANES = 2 * HEAD_DIM
CONV_DIM = 512
ATTN_DIM = N_HEADS * HEAD_DIM
CONV_WIDTH = 3
CMP_STRIDE = 16
CMP_LEN = 2 * CMP_STRIDE
SEL_BLOCK = 64
N_SEL = 16
WINDOW = 512
Q_BLOCK = 128
PAGE_SIZE = 128
N_PAGED = 4
RMS_EPS = 1e-6
NEG_INF = -1e30
FORCE_SCORE = 1e4
SCALE = HEAD_DIM ** -0.5
Q_SCALE = SCALE * 1.4426950408889634
GATE_PAD = 128
KEY_CHUNK = 512
VMEM_LIMIT = 56 * 1024 * 1024

F32 = jnp.float32
BF16 = jnp.bfloat16


def _rms(x, g):
    return x * lax.rsqrt(jnp.mean(x * x, axis=-1, keepdims=True) + RMS_EPS) * g


def _dot(a, b):
    return jnp.dot(a, b, preferred_element_type=F32)


def _dot_nt(a, b):
    return lax.dot_general(a, b, (((1,), (1,)), ((), ())), preferred_element_type=F32)


def _halves(dot, a, b):
    h = a.shape[0] // 2
    return jnp.concatenate([dot(a[:h], b), dot(a[h:], b)], axis=0)


def _params(*sem):
    return pltpu.CompilerParams(dimension_semantics=sem, vmem_limit_bytes=VMEM_LIMIT)


def _premix_body(seq_mode, tm, x_ref, g_ref, wc_ref, wq_ref, wkv_ref, wg_ref, cw_ref, gco_ref, *rest):
    if seq_mode:
        ycn_ref, q_ref, kvp_ref, kvw_ref, kvb_ref, gates_ref, tail_ref, ubuf = rest
    else:
        p1_ref, p2_ref, ycn_ref, q_ref, kvp_ref, kvw_ref, kvb_ref, gates_ref, tail_ref = rest
    x = x_ref[...]
    xn = _rms(x, g_ref[...]).astype(BF16)
    pc = _dot(xn, wc_ref[...])
    bg, u = pc[:, :CONV_DIM], pc[:, CONV_DIM:2 * CONV_DIM] * pc[:, 2 * CONV_DIM:]
    if seq_mode:
        @pl.when(pl.program_id(1) == 0)
        def _():
            ubuf[0:8, :] = jnp.zeros((8, CONV_DIM), F32)
        ubuf[8:8 + tm, :] = u
        up1 = ubuf[7:7 + tm, :]
        up2 = ubuf[6:6 + tm, :]
        ubuf[0:8, :] = ubuf[tm:tm + 8, :]
        tail_ref[...] = u[tm - 8:, :]
    else:
        up1, up2 = p1_ref[...], p2_ref[...]
        tail_ref[...] = u
    cw = cw_ref[...]
    yc = bg * (cw[0:1, :] * up2 + cw[1:2, :] * up1 + cw[2:3, :] * u)
    ycn_ref[...] = _rms(yc, gco_ref[...]).astype(BF16)
    q_ref[...] = (_dot(xn, wq_ref[...]) * Q_SCALE).astype(BF16)
    kv = _dot(xn, wkv_ref[...])
    kvp_ref[...] = kv[:, :N_PAGED * LANES]
    kvw_ref[...] = kv[:, N_PAGED * LANES:]
    kvb_ref[...] = kv[:, 2 * LANES:].astype(BF16)
    gates_ref[...] = jax.nn.sigmoid(_dot(xn, wg_ref[...]))


def _premix(x, layer, w, prev=None, tm=512):
    nb, s, d = x.shape
    seq_mode = prev is None
    if seq_mode:
        grid = (nb, s // tm)
        row = lambda c: pl.BlockSpec((None, tm, c), lambda b, i: (b, i, 0))
        tail_spec = pl.BlockSpec((None, 8, CONV_DIM), lambda b, i: (b, 0, 0))
        tail_shape = (nb, 8, CONV_DIM)
        xin = x
        lead = (nb, s)
    else:
        tm = nb
        grid = (1, 1)
        row = lambda c: pl.BlockSpec((tm, c), lambda b, i: (0, 0))
        tail_spec = row(CONV_DIM)
        tail_shape = (nb, CONV_DIM)
        xin = x.reshape(nb, d)
        lead = (nb,)
    wspec = lambda a: pl.BlockSpec((None,) + a.shape[1:], lambda b, i: (layer,) + (0,) * (a.ndim - 1))
    ws = [w["g_mix"], w["w_c"], w["w_q"], w["w_kv"], w["w_g"], w["conv_w"], w["g_conv_out"]]
    in_specs = [row(d)] + [wspec(a) for a in ws]
    args = [xin] + ws
    scratch = []
    if seq_mode:
        scratch = [pltpu.VMEM((tm + 8, CONV_DIM), F32)]
    else:
        in_specs += [row(CONV_DIM), row(CONV_DIM)]
        args += list(prev)
    out_shape = [
        jax.ShapeDtypeStruct(lead + (CONV_DIM,), BF16),
        jax.ShapeDtypeStruct(lead + (ATTN_DIM,), BF16),
        jax.ShapeDtypeStruct(lead + (N_PAGED * LANES,), F32),
        jax.ShapeDtypeStruct(lead + (2 * LANES,), F32),
        jax.ShapeDtypeStruct(lead + (4 * LANES,), BF16),
        jax.ShapeDtypeStruct(lead + (GATE_PAD,), F32),
        jax.ShapeDtypeStruct(tail_shape, F32),
    ]
    out_specs = [row(CONV_DIM), row(ATTN_DIM), row(N_PAGED * LANES), row(2 * LANES), row(4 * LANES),
                 row(GATE_PAD), tail_spec]
    return pl.pallas_call(
        functools.partial(_premix_body, seq_mode, tm),
        grid=grid, in_specs=in_specs, out_specs=out_specs, out_shape=out_shape,
        scratch_shapes=scratch, compiler_params=_params("arbitrary", "arbitrary"),
        name="premix_seq" if seq_mode else "premix_step",
    )(*args)


def _cmp_bias_body(pe_ref, w1_ref, o_ref):
    o_ref[...] = jnp.sum(pe_ref[...] * w1_ref[...], axis=0, keepdims=True)


def _cmp_bias(pe, w1):
    nl = pe.shape[0]
    k = CMP_LEN * HEAD_DIM
    return pl.pallas_call(
        _cmp_bias_body,
        grid=(nl, 2),
        in_specs=[pl.BlockSpec((None, None, k, 1), lambda l, t: (l, t, 0, 0)),
                  pl.BlockSpec((None, None, k, HEAD_DIM), lambda l, t: (l, t, 0, 0))],
        out_specs=pl.BlockSpec((None, None, 1, HEAD_DIM), lambda l, t: (l, t, 0, 0)),
        out_shape=jax.ShapeDtypeStruct((nl, 2, 1, HEAD_DIM), F32),
        compiler_params=_params("arbitrary", "arbitrary"), name="cmp_bias",
    )(pe, w1)


def _cmp_finish(acc, bias, w2, sbuf):
    n = acc.shape[0]
    sbuf[0:n, :] = acc[:, LANES:]
    sbuf[n:n + 8, :] = jnp.zeros((8, LANES), F32)
    h = jax.nn.gelu(acc[:, :LANES] + sbuf[1:n + 1, :] + bias)
    return _dot(h.astype(BF16), w2)


def _compress_body(nch, k_ref, v_ref, wk_ref, wv_ref, bk_ref, bv_ref, w2k_ref, w2v_ref, kc_ref, vc_ref, sbuf):
    for src, w1, b, w2, out in ((k_ref, wk_ref, bk_ref, w2k_ref, kc_ref), (v_ref, wv_ref, bv_ref, w2v_ref, vc_ref)):
        acc = jnp.zeros((nch, 2 * LANES), F32)
        for t in range(CMP_STRIDE // 2):
            lhs = jnp.concatenate([src[pl.ds(j, nch, stride=CMP_STRIDE), :] for j in (2 * t, 2 * t + 1)], axis=1)
            acc = acc + _dot(lhs.astype(BF16), w1[t])
        out[...] = _cmp_finish(acc, b[...], w2[...], sbuf).astype(BF16)


def _compress(kvp, layer, w):
    nb, s, _ = kvp.shape
    nch = s // CMP_STRIDE
    wspec = lambda a: pl.BlockSpec((None,) + a.shape[1:], lambda b: (layer,) + (0,) * (a.ndim - 1))
    ws = [w["cmp_w1_k"], w["cmp_w1_v"], w["cmp_b_k"], w["cmp_b_v"], w["cmp_w2_k"], w["cmp_w2_v"]]
    return pl.pallas_call(
        functools.partial(_compress_body, nch),
        grid=(nb,),
        in_specs=[pl.BlockSpec((None, s, LANES), lambda b: (b, 0, 0)),
                  pl.BlockSpec((None, s, LANES), lambda b: (b, 0, 1))] + [wspec(a) for a in ws],
        out_specs=[pl.BlockSpec((None, nch, LANES), lambda b: (b, 0, 0))] * 2,
        out_shape=[jax.ShapeDtypeStruct((nb, nch, LANES), BF16)] * 2,
        scratch_shapes=[pltpu.VMEM((nch + 8, LANES), F32)],
        compiler_params=_params("arbitrary"), name="compress",
    )(kvp, kvp, *ws)


def _half_mask(rows):
    return lax.broadcasted_iota(jnp.int32, (rows, LANES), 1) < HEAD_DIM


def _stack_q(qt):
    n = qt.shape[0]
    lo = _half_mask(n)
    zero = jnp.zeros((n, LANES), qt.dtype)
    parts = []
    for kvh in range(N_KV_HEADS):
        for g in range(GROUP):
            blk = qt[:, g * LANES:(g + 1) * LANES]
            parts.append(jnp.where(lo if kvh == 0 else ~lo, blk, zero))
    return jnp.concatenate(parts, axis=0)


def _unstack_o(o, n):
    lo = _half_mask(n)
    cols = [jnp.where(lo, o[g * n:(g + 1) * n, :], o[(GROUP + g) * n:(GROUP + g + 1) * n, :]) for g in range(GROUP)]
    return jnp.concatenate(cols, axis=1)


def _masked_softmax(s, mask):
    m = jnp.max(jnp.where(mask, s, NEG_INF), axis=-1, keepdims=True)
    e = jnp.where(mask, jnp.exp2(s - m), 0.0)
    l = jnp.sum(e, axis=-1, keepdims=True)
    return e / jnp.where(l > 0.0, l, 1.0)


def _add_tile_bias(s, bias):
    n = bias.shape[0]
    return jnp.concatenate([s[r * n:(r + 1) * n, :] + bias for r in range(s.shape[0] // n)], axis=0)


def _split_bf16(x, terms):
    out = []
    for _ in range(terms):
        hi = x.astype(BF16)
        out.append(hi)
        x = x - hi.astype(F32)
    return out


def _dot_exact_rhs(x, rhs_bf16, terms):
    return sum(_dot(t, rhs_bf16) for t in _split_bf16(x, terms))


def _combine(gates, expand, o_c, o_s, o_w, g_out):
    gx = _dot_exact_rhs(gates, expand, 2)
    ya = gx[:, :ATTN_DIM] * o_c + gx[:, ATTN_DIM:2 * ATTN_DIM] * o_s + gx[:, 2 * ATTN_DIM:] * o_w
    return _rms(ya, g_out).astype(BF16)


def _cmp_to_sel(nc_rows, ns):
    n = np.arange(nc_rows)[:, None]
    s = np.arange(LANES)[None, :]
    ov = (n * CMP_STRIDE <= s * SEL_BLOCK + SEL_BLOCK - 1) & (n * CMP_STRIDE + CMP_LEN - 1 >= s * SEL_BLOCK) & (s < ns)
    return jnp.asarray(ov.astype(np.float32), dtype=BF16)


def _gate_expand():
    e = np.zeros((GATE_PAD, 3 * ATTN_DIM), np.float32)
    for br in range(3):
        for kvh in range(N_KV_HEADS):
            for g in range(GROUP):
                c0 = br * ATTN_DIM + g * LANES + kvh * HEAD_DIM
                e[br * N_HEADS + kvh * GROUP + g, c0:c0 + HEAD_DIM] = 1.0
    return jnp.asarray(e, dtype=BF16)


def _sel_onehot(n_keys):
    t = np.arange(n_keys)[:, None] // SEL_BLOCK
    return jnp.asarray((t == np.arange(LANES)[None, :]).astype(np.float32), dtype=BF16)


def _chunk_perm():
    m = np.arange(PAGE_SIZE)[:, None]
    t = np.arange(PAGE_SIZE)[None, :]
    per_page = PAGE_SIZE // CMP_STRIDE
    return jnp.asarray((t == CMP_STRIDE * (m % per_page) + m // per_page).astype(np.float32), dtype=BF16)


def _rank_penalty(imp_t, ns):
    r = imp_t.shape[1]
    sub = lax.broadcasted_iota(jnp.int32, (8, r), 0)
    cnt = jnp.zeros((ns, r), F32)
    for sp in range(ns):
        other = imp_t[sp:sp + 1, :]
        parts = []
        for blk in range(ns // 8):
            t = imp_t[blk * 8:(blk + 1) * 8, :]
            if blk * 8 > sp:
                parts.append(jnp.where(other >= t, 1.0, 0.0))
            elif blk * 8 + 7 <= sp:
                parts.append(jnp.where(other > t, 1.0, 0.0))
            else:
                parts.append(jnp.where(sub > sp - blk * 8, jnp.where(other >= t, 1.0, 0.0), jnp.where(other > t, 1.0, 0.0)))
        cnt = cnt + jnp.concatenate(parts, axis=0)
    return jnp.where(cnt < N_SEL, 0.0, NEG_INF)


def _prompt_attn_body(ns, q_ref, kc_ref, vc_ref, ksl_ref, vsl_ref, kw_ref, vw_ref, oh_ref, ov_ref, gates_ref,
                      ex_ref, go_ref, out_ref, pen_s):
    i = pl.program_id(1)
    nq = Q_BLOCK
    rows = N_HEADS * nq
    qz = _stack_q(q_ref[...])
    qpos_t = i * nq + lax.broadcasted_iota(jnp.int32, (nq, 1), 0)

    kc = kc_ref[...]
    nc = kc.shape[0]
    c_end = lax.broadcasted_iota(jnp.int32, (nq, nc), 1) * CMP_STRIDE + (CMP_LEN - 1)
    s_c = _add_tile_bias(_halves(_dot_nt, qz, kc), jnp.where(c_end <= qpos_t, 0.0, NEG_INF))
    e_c = jnp.exp2(s_c - jnp.max(s_c, axis=-1, keepdims=True))
    any_c = jnp.where(qpos_t >= CMP_LEN - 1, 1.0, 0.0)
    p_c = e_c * (jnp.concatenate([any_c] * N_HEADS, axis=0) / jnp.sum(e_c, axis=-1, keepdims=True))
    o_c = _dot(p_c.astype(BF16), vc_ref[...])

    psum = jnp.concatenate(
        [sum(p_c[(kvh * GROUP + g) * nq:(kvh * GROUP + g + 1) * nq, :] for g in range(GROUP))
         for kvh in range(N_KV_HEADS)], axis=0)
    imp = _dot_exact_rhs(psum, ov_ref[...], 3)
    r2 = N_KV_HEADS * nq
    s_idx = lax.broadcasted_iota(jnp.int32, (r2, LANES), 1)
    q_blk = (i * nq + lax.broadcasted_iota(jnp.int32, (r2, LANES), 0) % nq) // SEL_BLOCK
    forced = (s_idx == 0) | (s_idx == q_blk) | (s_idx == q_blk - 1)
    imp = jnp.where(s_idx <= q_blk, jnp.where(forced, FORCE_SCORE, imp), -1.0)
    imp = jnp.where(s_idx < ns, imp, -2.0)
    pen_s[...] = jnp.zeros((LANES, r2), F32)

    @pl.when((i + 1) * nq > N_SEL * SEL_BLOCK)
    def _():
        pen_s[0:ns, :] = _rank_penalty(imp.T[0:ns, :], ns)

    pen = pen_s[...].T
    pen = jnp.where(s_idx <= q_blk, pen, jnp.where(s_idx < ns, NEG_INF, 0.0)).astype(BF16)
    pen_rows = jnp.concatenate([pen[kvh * nq:(kvh + 1) * nq, :] for kvh in range(N_KV_HEADS) for _ in range(GROUP)],
                               axis=0)
    q_aug = jnp.concatenate([qz, pen_rows], axis=1)

    ones_v = jnp.ones((KEY_CHUNK, LANES), BF16)

    def scores(c):
        k0 = pl.multiple_of(c * KEY_CHUNK, KEY_CHUNK)
        k_aug = jnp.concatenate([ksl_ref[pl.ds(k0, KEY_CHUNK), :], oh_ref[pl.ds(k0, KEY_CHUNK), :]], axis=1)
        return _halves(_dot_nt, q_aug, k_aug).astype(BF16)

    def update(c, s, m_old, acc):
        k0 = pl.multiple_of(c * KEY_CHUNK, KEY_CHUNK)
        m_new = jnp.maximum(m_old, jnp.max(s, axis=-1, keepdims=True).astype(F32))
        p = jnp.exp2(s - m_new.astype(BF16))
        v_aug = jnp.concatenate([vsl_ref[pl.ds(k0, KEY_CHUNK), :], ones_v], axis=1)
        return m_new, jnp.exp2(m_old - m_new) * acc + _dot(p, v_aug)

    last = (i * nq) // KEY_CHUNK
    carry = (jnp.full((rows, 1), NEG_INF, F32), jnp.zeros((rows, 2 * LANES), F32))
    m_old, acc = lax.fori_loop(0, last, lambda c, cr: update(c, scores(c), *cr), carry)
    kpos = last * KEY_CHUNK + lax.broadcasted_iota(jnp.int32, (nq, KEY_CHUNK), 1)
    _, acc = update(last, _add_tile_bias(scores(last), jnp.where(kpos <= qpos_t, 0.0, NEG_INF).astype(BF16)), m_old, acc)
    o_s = acc[:, :LANES] / acc[:, LANES:]

    nw = WINDOW + nq
    w0 = pl.multiple_of(jnp.maximum(i * nq - WINDOW, 0), nq)
    diff = qpos_t - (w0 + lax.broadcasted_iota(jnp.int32, (nq, nw), 1))
    s_w = _add_tile_bias(_halves(_dot_nt, qz, kw_ref[pl.ds(w0, nw), :]).astype(BF16),
                         jnp.where((diff >= 0) & (diff < WINDOW), 0.0, NEG_INF).astype(BF16))
    e_w = jnp.exp2(s_w - jnp.max(s_w, axis=-1, keepdims=True))
    ow_aug = _dot(e_w, jnp.concatenate([vw_ref[pl.ds(w0, nw), :], jnp.ones((nw, LANES), BF16)], axis=1))
    o_w = ow_aug[:, :LANES] / ow_aug[:, LANES:]

    out_ref[...] = _combine(gates_ref[...], ex_ref[...], _unstack_o(o_c, nq), _unstack_o(o_s, nq),
                            _unstack_o(o_w, nq), go_ref[...])


def _prompt_attn(q, kc, vc, kvb, gates, layer, w, consts):
    nb, s, _ = q.shape
    nc = kc.shape[1]
    ns = s // SEL_BLOCK
    col = lambda c: pl.BlockSpec((None, s, LANES), lambda b, i: (b, 0, c))
    whole = lambda a: pl.BlockSpec(a.shape, lambda b, i: (0,) * a.ndim)
    return pl.pallas_call(
        functools.partial(_prompt_attn_body, ns),
        grid=(nb, s // Q_BLOCK),
        in_specs=[pl.BlockSpec((None, Q_BLOCK, ATTN_DIM), lambda b, i: (b, i, 0)),
                  pl.BlockSpec((None, nc, LANES), lambda b, i: (b, 0, 0)),
                  pl.BlockSpec((None, nc, LANES), lambda b, i: (b, 0, 0)),
                  col(0), col(1), col(2), col(3),
                  whole(consts["onehot"]), whole(consts["ov"]),
                  pl.BlockSpec((None, Q_BLOCK, GATE_PAD), lambda b, i: (b, i, 0)),
                  whole(consts["expand"]),
                  pl.BlockSpec((None, 1, ATTN_DIM), lambda b, i: (layer, 0, 0))],
        out_specs=pl.BlockSpec((None, Q_BLOCK, ATTN_DIM), lambda b, i: (b, i, 0)),
        out_shape=jax.ShapeDtypeStruct((nb, s, ATTN_DIM), BF16),
        scratch_shapes=[pltpu.VMEM((LANES, N_KV_HEADS * Q_BLOCK), F32)],
        compiler_params=_params("arbitrary", "arbitrary"), name="prompt_attn",
    )(q, kc, vc, kvb, kvb, kvb, kvb, consts["onehot"], consts["ov"], gates, consts["expand"], w["g_attn_out"])


def _sample_attn_body(n_pages, nseq, pt_ref, q_ref, kvp_ref, kvw_ref, gates_ref, *rest):
    n_pg = nseq * n_pages
    pages = rest[:n_pg]
    (win_ref, wk_ref, wv_ref, bk_ref, bv_ref, w2k_ref, w2v_ref, perm_ref, oh_ref, ov_ref, ex_ref,
     go_ref) = rest[n_pg:n_pg + 12]
    out_ref, wout_ref, sbuf = rest[-3:]
    past = n_pages * PAGE_SIZE
    qpos = past
    nch = past // CMP_STRIDE
    per_page = PAGE_SIZE // CMP_STRIDE
    seqs = range(nseq)
    R = nseq * N_HEADS
    perm = perm_ref[...]

    seq_pages = [pages[sq * n_pages:(sq + 1) * n_pages] for sq in seqs]
    xs = []
    for pgs in seq_pages:
        x_all = _dot_nt(perm, jnp.concatenate([pg[0:2 * LANES, :] for pg in pgs], axis=0).astype(BF16))
        xs += [x_all[:, p * 2 * LANES:(p + 1) * 2 * LANES] for p in range(n_pages)]

    def compress(c0, w1, b, w2):
        acc = jnp.zeros((nseq * nch, 2 * LANES), F32)
        for t in range(CMP_STRIDE // 2):
            lhs = jnp.concatenate(
                [jnp.concatenate([x[j * per_page:(j + 1) * per_page, c0:c0 + LANES] for j in (2 * t, 2 * t + 1)], axis=1)
                 for x in xs], axis=0)
            acc = acc + _dot(lhs.astype(BF16), w1[t])
        c = _cmp_finish(acc, b[...], w2[...], sbuf).astype(BF16)
        return [c[sq * nch:(sq + 1) * nch, :] for sq in seqs]

    kcs = compress(0, wk_ref, bk_ref, w2k_ref)
    vcs = compress(LANES, wv_ref, bv_ref, w2v_ref)

    ns = (past + 1 + SEL_BLOCK - 1) // SEL_BLOCK
    q_blk = qpos // SEL_BLOCK
    w_len = win_ref.shape[2]
    lane = lax.broadcasted_iota(jnp.int32, (R, LANES), 1)
    rows_of = lambda x, sq: x[sq * N_HEADS:(sq + 1) * N_HEADS, :]
    per_row = lambda x: jnp.concatenate([jnp.broadcast_to(x[sq:sq + 1, :], (N_HEADS, x.shape[1])) for sq in seqs], axis=0)
    as_bf16 = lambda x: x.astype(BF16).astype(F32)
    qzs = [_stack_q(q_ref[sq:sq + 1, :]) for sq in seqs]
    qz = jnp.concatenate(qzs, axis=0)
    qf = qz.astype(F32)
    kv_new = per_row(kvp_ref[...])
    kvw_new = per_row(kvw_ref[...])

    s_c = jnp.concatenate([_dot_nt(qzs[sq], kcs[sq]) for sq in seqs], axis=0)
    blk = lax.broadcasted_iota(jnp.int32, (R, nch), 1)
    p_c = _masked_softmax(s_c, (blk * CMP_STRIDE + (CMP_LEN - 1) <= qpos) & (blk < nch - 1))
    p_cb = p_c.astype(BF16)
    o_c = jnp.concatenate([_dot(rows_of(p_cb, sq), vcs[sq]) for sq in seqs], axis=0)

    psum = jnp.concatenate(
        [jnp.sum(p_c[sq * N_HEADS + kvh * GROUP:sq * N_HEADS + (kvh + 1) * GROUP, :], axis=0, keepdims=True)
         if kvh < N_KV_HEADS else jnp.zeros((N_HEADS - N_KV_HEADS, nch), F32)
         for sq in seqs for kvh in range(N_KV_HEADS + 1)], axis=0)
    imp = _dot_exact_rhs(psum, ov_ref[...], 3)
    forced = (lane == 0) | (lane == q_blk) | (lane == q_blk - 1)
    imp = jnp.where(lane <= q_blk, jnp.where(forced, FORCE_SCORE, imp), -1.0)
    imp = jnp.where(lane < ns, imp, -2.0)
    imp_t = imp.T
    sub = lax.broadcasted_iota(jnp.int32, (LANES, LANES), 0)
    lan = lax.broadcasted_iota(jnp.int32, (LANES, LANES), 1)
    pens = []
    for sq in seqs:
        for kvh in range(N_KV_HEADS):
            r = sq * N_HEADS + kvh
            other = imp_t[:, r:r + 1]
            mine = imp[r:r + 1, :]
            beats = (other > mine) | ((other == mine) & (sub < lan))
            cnt = jnp.sum(beats.astype(jnp.int32), axis=0, keepdims=True)
            sel = (cnt < N_SEL) & (lane[0:1, :] <= q_blk)
            pens += [jnp.where(sel | (lane[0:1, :] >= ns), 0.0, NEG_INF)] * GROUP
    pen = jnp.concatenate(pens, axis=0)
    q_aug = jnp.concatenate([qz, pen.astype(BF16)], axis=1)

    tokens = lambda pgs, r0: jnp.concatenate([pg[r0:r0 + LANES, :] for pg in pgs], axis=1).astype(BF16)
    s_s = jnp.concatenate(
        [_dot(rows_of(q_aug, sq), jnp.concatenate([tokens(seq_pages[sq], 2 * LANES), oh_ref[...]], axis=0))
         for sq in seqs], axis=0)
    pen_new = jnp.sum(jnp.where(lane == q_blk, pen, 0.0), axis=-1, keepdims=True)
    s_new = jnp.sum(qf * as_bf16(kv_new[:, 2 * LANES:3 * LANES]), axis=-1, keepdims=True) + pen_new
    m = jnp.maximum(jnp.max(s_s, axis=-1, keepdims=True), s_new)
    p_s = jnp.exp2(s_s - m)
    p_new = jnp.exp2(s_new - m)
    den = jnp.sum(p_s, axis=-1, keepdims=True) + p_new
    p_sb = p_s.astype(BF16)
    pv = [_dot_nt(rows_of(p_sb, sq), tokens(seq_pages[sq], 3 * LANES)) for sq in seqs]
    o_s = (jnp.concatenate(pv, axis=0) + as_bf16(p_new) * as_bf16(kv_new[:, 3 * LANES:4 * LANES])) / den

    wins = [win_ref[sq] for sq in seqs]
    s_w = jnp.concatenate([_dot(qzs[sq], wins[sq][:LANES, :].astype(BF16)) for sq in seqs], axis=0)
    diff = qpos - (past - w_len + lax.broadcasted_iota(jnp.int32, (R, w_len), 1))
    ok = (diff >= 0) & (diff < WINDOW)
    s_wn = jnp.sum(qf * as_bf16(kvw_new[:, :LANES]), axis=-1, keepdims=True)
    m = jnp.maximum(jnp.max(jnp.where(ok, s_w, NEG_INF), axis=-1, keepdims=True), s_wn)
    p_w = jnp.where(ok, jnp.exp2(s_w - m), 0.0)
    p_wn = jnp.exp2(s_wn - m)
    den = jnp.sum(p_w, axis=-1, keepdims=True) + p_wn
    p_wb = p_w.astype(BF16)
    o_w = (jnp.concatenate([_dot_nt(rows_of(p_wb, sq), wins[sq][LANES:, :].astype(BF16)) for sq in seqs], axis=0)
           + as_bf16(p_wn) * as_bf16(kvw_new[:, LANES:])) / den

    o_c, o_s, o_w = (jnp.concatenate([_unstack_o(rows_of(o, sq), 1) for sq in seqs], axis=0) for o in (o_c, o_s, o_w))
    out_ref[...] = _combine(gates_ref[...], ex_ref[...], o_c, o_s, o_w, go_ref[...])

    eye_w = (lax.broadcasted_iota(jnp.int32, (2 * LANES, 2 * LANES), 0)
             == lax.broadcasted_iota(jnp.int32, (2 * LANES, 2 * LANES), 1))
    last_lane = lax.broadcasted_iota(jnp.int32, (2 * LANES, w_len), 1) == w_len - 1
    for sq in seqs:
        new_col = jnp.sum(jnp.where(eye_w, kvw_ref[sq:sq + 1, :], 0.0), axis=1, keepdims=True)
        wout_ref[sq] = jnp.where(last_lane, new_col, pltpu.roll(wins[sq], w_len - 1, 1))


def _sample_attn(q, kvp, kvw, gates, cache_kv_t, cache_win_t, page_table, layer, w, consts, win_all=None, nseq=4):
    db, n_pages = page_table.shape
    nl, w_len = cache_win_t.shape[0], cache_win_t.shape[3]
    nch = n_pages * PAGE_SIZE // CMP_STRIDE
    assert db % nseq == 0
    r3 = lambda a: a.reshape(db // nseq, nseq, a.shape[-1])
    rowspec = lambda c: pl.BlockSpec((None, nseq, c), lambda b, pt: (b, 0, 0))
    whole = lambda a: pl.BlockSpec(a.shape, lambda b, pt: (0,) * a.ndim)
    wspec = lambda a: pl.BlockSpec((None,) + a.shape[1:], lambda b, pt: (layer,) + (0,) * (a.ndim - 1))
    page_specs = [pl.BlockSpec((None, None, N_PAGED * LANES, PAGE_SIZE),
                               functools.partial(lambda col, b, pt: (layer, pt[b, col], 0, 0), sq * n_pages + p))
                  for sq in range(nseq) for p in range(n_pages)]
    ws = [w["cmp_w1_k"], w["cmp_w1_v"], w["cmp_b_k"], w["cmp_b_v"], w["cmp_w2_k"], w["cmp_w2_v"]]
    cs = [consts["perm"], consts["onehot_s"], consts["ov_s"], consts["expand"]]
    win_spec = pl.BlockSpec((None, nseq, 2 * LANES, w_len), lambda b, pt: (layer, b, 0, 0))
    args = [page_table.reshape(db // nseq, nseq * n_pages), r3(q), r3(kvp), r3(kvw), r3(gates), *([cache_kv_t] * (nseq * n_pages)), cache_win_t, *ws, *cs,
            w["g_attn_out"]]
    in_specs = ([rowspec(ATTN_DIM), rowspec(N_PAGED * LANES), rowspec(2 * LANES), rowspec(GATE_PAD)] + page_specs
                + [win_spec] + [wspec(a) for a in ws] + [whole(a) for a in cs] + [wspec(w["g_attn_out"])])
    aliases = {}
    if win_all is not None:
        in_specs.append(pl.BlockSpec(memory_space=pl.ANY))
        args.append(win_all)
        aliases = {len(args) - 1: 1}
    grid_spec = pltpu.PrefetchScalarGridSpec(
        num_scalar_prefetch=1, grid=(db // nseq,), in_specs=in_specs, out_specs=[rowspec(ATTN_DIM), win_spec],
        scratch_shapes=[pltpu.VMEM((nseq * nch + 8, LANES), F32)],
    )
    out, win_all = pl.pallas_call(
        functools.partial(_sample_attn_body, n_pages, nseq),
        grid_spec=grid_spec,
        out_shape=[jax.ShapeDtypeStruct((db // nseq, nseq, ATTN_DIM), BF16),
                   jax.ShapeDtypeStruct((nl, db, 2 * LANES, w_len), F32)],
        input_output_aliases=aliases,
        compiler_params=_params("arbitrary"), name="sample_attn",
    )(*args)
    return out.reshape(db, ATTN_DIM), win_all


def _postmix_body(final, x_ref, yc_ref, ya_ref, woc_ref, woa_ref, gf_ref, wg_ref, wu_ref, wd_ref, gl_ref, xo_ref, *rest):
    h_s, acc_s = rest[-2:]
    f = pl.program_id(1)

    @pl.when(f == 0)
    def _():
        x1 = x_ref[...] + _dot(yc_ref[...], woc_ref[...]) + _dot(ya_ref[...], woa_ref[...])
        h_s[...] = _rms(x1, gf_ref[...]).astype(BF16)
        acc_s[...] = x1

    h = h_s[...]
    a = _dot(h, wg_ref[...])
    act = (a * jax.nn.sigmoid(a) * _dot(h, wu_ref[...])).astype(BF16)
    acc_s[...] += _dot(act, wd_ref[...])

    @pl.when(f == pl.num_programs(1) - 1)
    def _():
        out = acc_s[...]
        xo_ref[...] = out
        if final:
            rest[0][...] = _rms(out, gl_ref[...])


def _postmix(x, ycn, yan, layer, w, final, tm=512, n_ff=2):
    t, d = x.shape
    tm = min(tm, t)
    d_ff = w["w_gate"].shape[-1]
    tf = d_ff // n_ff
    assert tf * n_ff == d_ff and tf % LANES == 0
    row = lambda c: pl.BlockSpec((tm, c), lambda i, f: (i, 0))
    wspec = lambda a: pl.BlockSpec((None,) + a.shape[1:], lambda i, f: (layer,) + (0,) * (a.ndim - 1))
    gl = w["g_final"]
    n_out = 2 if final else 1
    outs = pl.pallas_call(
        functools.partial(_postmix_body, final),
        grid=(t // tm, n_ff),
        in_specs=[row(d), row(CONV_DIM), row(ATTN_DIM), wspec(w["w_out_c"]), wspec(w["w_out_a"]), wspec(w["g_ffn"]),
                  pl.BlockSpec((None, d, tf), lambda i, f: (layer, 0, f)),
                  pl.BlockSpec((None, d, tf), lambda i, f: (layer, 0, f)),
                  pl.BlockSpec((None, tf, d), lambda i, f: (layer, f, 0)),
                  pl.BlockSpec(gl.shape, lambda i, f: (0, 0))],
        out_specs=[row(d)] * n_out,
        out_shape=[jax.ShapeDtypeStruct((t, d), F32)] * n_out,
        scratch_shapes=[pltpu.VMEM((tm, d), BF16), pltpu.VMEM((tm, d), F32)],
        compiler_params=_params("arbitrary", "arbitrary"), name="postmix",
    )(x, ycn, yan, w["w_out_c"], w["w_out_a"], w["g_ffn"], w["w_gate"], w["w_up"], w["w_down"], gl)
    return outs


def _prep_weights(g_mix, w_in, conv_w, cmp_pe_k, cmp_w1_k, cmp_w2_k, cmp_pe_v, cmp_w1_v, cmp_w2_v, g_conv_out,
                  g_attn_out, w_out, g_ffn, w_gate, w_up, w_down, g_final):
    nl, d, _ = w_in.shape
    c3 = 3 * CONV_DIM
    kv_cols = 6 * LANES

    def perm_heads(a, axis):
        shp = a.shape
        a = a.reshape(shp[:axis] + (N_KV_HEADS, GROUP, HEAD_DIM) + shp[axis + 1:])
        a = jnp.swapaxes(a, axis, axis + 1)
        return a.reshape(shp)

    def blockdiag2(a):
        z = jnp.zeros_like(a)
        return jnp.concatenate([jnp.concatenate([a, z], axis=-1), jnp.concatenate([z, a], axis=-1)], axis=-2)

    def cmp_w1(w1):
        a = blockdiag2(w1[:, :CMP_STRIDE])
        b = blockdiag2(w1[:, CMP_STRIDE:])
        return jnp.concatenate([a, b], axis=-1).astype(BF16).reshape(nl, CMP_STRIDE // 2, 2 * LANES, 2 * LANES)

    pe = jnp.stack([cmp_pe_k, cmp_pe_v], axis=1).reshape(nl, 2, CMP_LEN * HEAD_DIM, 1)
    w1 = jnp.stack([cmp_w1_k, cmp_w1_v], axis=1).reshape(nl, 2, CMP_LEN * HEAD_DIM, HEAD_DIM)
    bias = _cmp_bias(pe, w1)
    bias2 = jnp.concatenate([bias, bias], axis=-1)
    gates_w = jnp.pad(w_in[:, :, c3 + ATTN_DIM + kv_cols:], ((0, 0), (0, 0), (0, GATE_PAD - 3 * N_HEADS)))
    return {
        "g_mix": g_mix.reshape(nl, 1, d),
        "w_c": w_in[:, :, :c3].astype(BF16),
        "w_q": perm_heads(w_in[:, :, c3:c3 + ATTN_DIM], 2).astype(BF16),
        "w_kv": w_in[:, :, c3 + ATTN_DIM:c3 + ATTN_DIM + kv_cols].astype(BF16),
        "w_g": gates_w.astype(BF16),
        "conv_w": conv_w,
        "g_conv_out": g_conv_out.reshape(nl, 1, CONV_DIM),
        "cmp_w1_k": cmp_w1(cmp_w1_k), "cmp_w1_v": cmp_w1(cmp_w1_v),
        "cmp_b_k": bias2[:, 0], "cmp_b_v": bias2[:, 1],
        "cmp_w2_k": blockdiag2(cmp_w2_k).astype(BF16), "cmp_w2_v": blockdiag2(cmp_w2_v).astype(BF16),
        "g_attn_out": perm_heads(g_attn_out, 1).reshape(nl, 1, ATTN_DIM),
        "w_out_c": w_out[:, :CONV_DIM].astype(BF16),
        "w_out_a": perm_heads(w_out[:, CONV_DIM:], 1).astype(BF16),
        "g_ffn": g_ffn.reshape(nl, 1, d),
        "w_gate": w_gate.astype(BF16), "w_up": w_up.astype(BF16), "w_down": w_down.astype(BF16),
        "g_final": g_final.reshape(1, d),
    }


def kernel(x_prompt, x_sample, cache_kv, cache_win, state_conv, page_table, g_mix, w_in, conv_w, cmp_pe_k, cmp_w1_k, cmp_w2_k, cmp_pe_v, cmp_w1_v, cmp_w2_v, g_conv_out, g_attn_out, w_out, g_ffn, w_gate, w_up, w_down, g_final):
    nl = w_in.shape[0]
    b, s, d = x_prompt.shape
    db, sq = x_sample.shape[:2]
    assert sq == 1 and s % 512 == 0
    n_pages = page_table.shape[1]
    past = n_pages * PAGE_SIZE
    w_len = cache_win.shape[2]
    w_keep = min(WINDOW, s)
    w = _prep_weights(g_mix, w_in, conv_w, cmp_pe_k, cmp_w1_k, cmp_w2_k, cmp_pe_v, cmp_w1_v, cmp_w2_v, g_conv_out,
                      g_attn_out, w_out, g_ffn, w_gate, w_up, w_down, g_final)
    consts = {
        "expand": _gate_expand(),
        "onehot": _sel_onehot(s), "ov": _cmp_to_sel(s // CMP_STRIDE, s // SEL_BLOCK),
        "onehot_s": _sel_onehot(past).T, "ov_s": _cmp_to_sel(past // CMP_STRIDE, (past + 1 + SEL_BLOCK - 1) // SEL_BLOCK),
        "perm": _chunk_perm(),
    }
    cache_kv = cache_kv.transpose(0, 1, 3, 4, 5, 2).reshape(nl, cache_kv.shape[1], N_PAGED * LANES, PAGE_SIZE)
    cache_win = cache_win.transpose(0, 1, 3, 4, 5, 2).reshape(nl, db, 2 * LANES, w_len)

    xp, xs = x_prompt, x_sample
    kvp_l, winp_l, convp_l, kvs_l, convs_l = [], [], [], [], []
    win_all = None
    for l in range(nl):
        final = l == nl - 1
        ycn, q, kvp, kvw, kvb, gates, tail = _premix(xp, l, w)
        kc, vc = _compress(kvp, l, w)
        yan = _prompt_attn(q, kc, vc, kvb, gates, l, w, consts)
        outs = _postmix(xp.reshape(b * s, d), ycn.reshape(b * s, -1), yan.reshape(b * s, -1), l, w, final)
        xp = outs[0].reshape(b, s, d)
        if final:
            y_prompt = outs[1].reshape(b, s, d)
        kvp_l.append(kvp)
        winp_l.append(kvw[:, s - w_keep:])
        convp_l.append(tail[:, 8 - (CONV_WIDTH - 1):])
        ycn, q, kvp, kvw, kvb, gates, u = _premix(xs, l, w, prev=(state_conv[l, :, 1], state_conv[l, :, 0]))
        yan, win_all = _sample_attn(q, kvp, kvw, gates, cache_kv, cache_win, page_table, l, w, consts, win_all)
        outs = _postmix(xs.reshape(db, d), ycn, yan, l, w, final)
        xs = outs[0].reshape(db, 1, d)
        if final:
            y_sample = outs[1].reshape(db, 1, d)
        kvs_l.append(kvp)
        convs_l.append(jnp.stack([state_conv[l, :, 1], u], axis=1))
    kv_shape = (N_PAGED, N_KV_HEADS, HEAD_DIM)
    return (y_prompt, y_sample,
            jnp.stack(kvp_l).reshape((nl, b, s) + kv_shape),
            jnp.stack(winp_l).reshape((nl, b, w_keep, 2) + kv_shape[1:]),
            jnp.stack(convp_l),
            jnp.stack(kvs_l).reshape((nl, db, 1) + kv_shape),
            win_all.reshape((nl, db, 2) + kv_shape[1:] + (w_len,)).transpose(0, 1, 5, 2, 3, 4),
            jnp.stack(convs_l))
```

```python
import functools

import numpy as np
import jax
import jax.numpy as jnp
from jax import lax
from jax.experimental import pallas as pl
from jax.experimental.pallas import tpu as pltpu

HEAD_DIM = 64
N_KV_HEADS = 2
GROUP = 4
N_HEADS = N_KV_HEADS * GROUP
LANES = 2 * HEAD_DIM
CONV_DIM = 512
ATTN_DIM = N_HEADS * HEAD_DIM
CONV_WIDTH = 3
CMP_STRIDE = 16
CMP_LEN = 2 * CMP_STRIDE
SEL_BLOCK = 64
N_SEL = 16
WINDOW = 512
Q_BLOCK = 128
PAGE_SIZE = 128
N_PAGED = 4
RMS_EPS = 1e-6
NEG_INF = -1e30
FORCE_SCORE = 1e4
SCALE = HEAD_DIM ** -0.5
Q_SCALE = SCALE * 1.4426950408889634
GATE_PAD = 128
KEY_CHUNK = 512
VMEM_LIMIT = 56 * 1024 * 1024

F32 = jnp.float32
BF16 = jnp.bfloat16


def _rms(x, g):
    return x * lax.rsqrt(jnp.mean(x * x, axis=-1, keepdims=True) + RMS_EPS) * g


def _dot(a, b):
    return jnp.dot(a, b, preferred_element_type=F32)


def _dot_nt(a, b):
    return lax.dot_general(a, b, (((1,), (1,)), ((), ())), preferred_element_type=F32)


def _halves(dot, a, b):
    h = a.shape[0] // 2
    return jnp.concatenate([dot(a[:h], b), dot(a[h:], b)], axis=0)


def _params(*sem):
    return pltpu.CompilerParams(dimension_semantics=sem, vmem_limit_bytes=VMEM_LIMIT)


def _premix_body(seq_mode, tm, x_ref, g_ref, wc_ref, wq_ref, wkv_ref, wg_ref, cw_ref, gco_ref, *rest):
    if seq_mode:
        ycn_ref, q_ref, kvp_ref, kvw_ref, kvb_ref, gates_ref, tail_ref, ubuf = rest
    else:
        p1_ref, p2_ref, ycn_ref, q_ref, kvp_ref, kvw_ref, kvb_ref, gates_ref, tail_ref = rest
    x = x_ref[...]
    xn = _rms(x, g_ref[...]).astype(BF16)
    pc = _dot(xn, wc_ref[...])
    bg, u = pc[:, :CONV_DIM], pc[:, CONV_DIM:2 * CONV_DIM] * pc[:, 2 * CONV_DIM:]
    if seq_mode:
        @pl.when(pl.program_id(1) == 0)
        def _():
            ubuf[0:8, :] = jnp.zeros((8, CONV_DIM), F32)
        ubuf[8:8 + tm, :] = u
        up1 = ubuf[7:7 + tm, :]
        up2 = ubuf[6:6 + tm, :]
        ubuf[0:8, :] = ubuf[tm:tm + 8, :]
        tail_ref[...] = u[tm - 8:, :]
    else:
        up1, up2 = p1_ref[...], p2_ref[...]
        tail_ref[...] = u
    cw = cw_ref[...]
    yc = bg * (cw[0:1, :] * up2 + cw[1:2, :] * up1 + cw[2:3, :] * u)
    ycn_ref[...] = _rms(yc, gco_ref[...]).astype(BF16)
    q_ref[...] = (_dot(xn, wq_ref[...]) * Q_SCALE).astype(BF16)
    kv = _dot(xn, wkv_ref[...])
    kvp_ref[...] = kv[:, :N_PAGED * LANES]
    kvw_ref[...] = kv[:, N_PAGED * LANES:]
    kvb_ref[...] = kv[:, 2 * LANES:].astype(BF16)
    gates_ref[...] = jax.nn.sigmoid(_dot(xn, wg_ref[...]))


def _premix(x, layer, w, prev=None, tm=512):
    nb, s, d = x.shape
    seq_mode = prev is None
    if seq_mode:
        grid = (nb, s // tm)
        row = lambda c: pl.BlockSpec((None, tm, c), lambda b, i: (b, i, 0))
        tail_spec = pl.BlockSpec((None, 8, CONV_DIM), lambda b, i: (b, 0, 0))
        tail_shape = (nb, 8, CONV_DIM)
        xin = x
        lead = (nb, s)
    else:
        tm = nb
        grid = (1, 1)
        row = lambda c: pl.BlockSpec((tm, c), lambda b, i: (0, 0))
        tail_spec = row(CONV_DIM)
        tail_shape = (nb, CONV_DIM)
        xin = x.reshape(nb, d)
        lead = (nb,)
    wspec = lambda a: pl.BlockSpec((None,) + a.shape[1:], lambda b, i: (layer,) + (0,) * (a.ndim - 1))
    ws = [w["g_mix"], w["w_c"], w["w_q"], w["w_kv"], w["w_g"], w["conv_w"], w["g_conv_out"]]
    in_specs = [row(d)] + [wspec(a) for a in ws]
    args = [xin] + ws
    scratch = []
    if seq_mode:
        scratch = [pltpu.VMEM((tm + 8, CONV_DIM), F32)]
    else:
        in_specs += [row(CONV_DIM), row(CONV_DIM)]
        args += list(prev)
    out_shape = [
        jax.ShapeDtypeStruct(lead + (CONV_DIM,), BF16),
        jax.ShapeDtypeStruct(lead + (ATTN_DIM,), BF16),
        jax.ShapeDtypeStruct(lead + (N_PAGED * LANES,), F32),
        jax.ShapeDtypeStruct(lead + (2 * LANES,), F32),
        jax.ShapeDtypeStruct(lead + (4 * LANES,), BF16),
        jax.ShapeDtypeStruct(lead + (GATE_PAD,), F32),
        jax.ShapeDtypeStruct(tail_shape, F32),
    ]
    out_specs = [row(CONV_DIM), row(ATTN_DIM), row(N_PAGED * LANES), row(2 * LANES), row(4 * LANES),
                 row(GATE_PAD), tail_spec]
    return pl.pallas_call(
        functools.partial(_premix_body, seq_mode, tm),
        grid=grid, in_specs=in_specs, out_specs=out_specs, out_shape=out_shape,
        scratch_shapes=scratch, compiler_params=_params("arbitrary", "arbitrary"),
        name="premix_seq" if seq_mode else "premix_step",
    )(*args)


def _cmp_bias_body(pe_ref, w1_ref, o_ref):
    o_ref[...] = jnp.sum(pe_ref[...] * w1_ref[...], axis=0, keepdims=True)


def _cmp_bias(pe, w1):
    nl = pe.shape[0]
    k = CMP_LEN * HEAD_DIM
    return pl.pallas_call(
        _cmp_bias_body,
        grid=(nl, 2),
        in_specs=[pl.BlockSpec((None, None, k, 1), lambda l, t: (l, t, 0, 0)),
                  pl.BlockSpec((None, None, k, HEAD_DIM), lambda l, t: (l, t, 0, 0))],
        out_specs=pl.BlockSpec((None, None, 1, HEAD_DIM), lambda l, t: (l, t, 0, 0)),
        out_shape=jax.ShapeDtypeStruct((nl, 2, 1, HEAD_DIM), F32),
        compiler_params=_params("arbitrary", "arbitrary"), name="cmp_bias",
    )(pe, w1)


def _cmp_finish(acc, bias, w2, sbuf):
    n = acc.shape[0]
    sbuf[0:n, :] = acc[:, LANES:]
    sbuf[n:n + 8, :] = jnp.zeros((8, LANES), F32)
    h = jax.nn.gelu(acc[:, :LANES] + sbuf[1:n + 1, :] + bias)
    return _dot(h.astype(BF16), w2)


def _compress_body(nch, k_ref, v_ref, wk_ref, wv_ref, bk_ref, bv_ref, w2k_ref, w2v_ref, kc_ref, vc_ref, sbuf):
    for src, w1, b, w2, out in ((k_ref, wk_ref, bk_ref, w2k_ref, kc_ref), (v_ref, wv_ref, bv_ref, w2v_ref, vc_ref)):
        acc = jnp.zeros((nch, 2 * LANES), F32)
        for t in range(CMP_STRIDE // 2):
            lhs = jnp.concatenate([src[pl.ds(j, nch, stride=CMP_STRIDE), :] for j in (2 * t, 2 * t + 1)], axis=1)
            acc = acc + _dot(lhs.astype(BF16), w1[t])
        out[...] = _cmp_finish(acc, b[...], w2[...], sbuf).astype(BF16)


def _compress(kvp, layer, w):
    nb, s, _ = kvp.shape
    nch = s // CMP_STRIDE
    wspec = lambda a: pl.BlockSpec((None,) + a.shape[1:], lambda b: (layer,) + (0,) * (a.ndim - 1))
    ws = [w["cmp_w1_k"], w["cmp_w1_v"], w["cmp_b_k"], w["cmp_b_v"], w["cmp_w2_k"], w["cmp_w2_v"]]
    return pl.pallas_call(
        functools.partial(_compress_body, nch),
        grid=(nb,),
        in_specs=[pl.BlockSpec((None, s, LANES), lambda b: (b, 0, 0)),
                  pl.BlockSpec((None, s, LANES), lambda b: (b, 0, 1))] + [wspec(a) for a in ws],
        out_specs=[pl.BlockSpec((None, nch, LANES), lambda b: (b, 0, 0))] * 2,
        out_shape=[jax.ShapeDtypeStruct((nb, nch, LANES), BF16)] * 2,
        scratch_shapes=[pltpu.VMEM((nch + 8, LANES), F32)],
        compiler_params=_params("arbitrary"), name="compress",
    )(kvp, kvp, *ws)


def _half_mask(rows):
    return lax.broadcasted_iota(jnp.int32, (rows, LANES), 1) < HEAD_DIM


def _stack_q(qt):
    n = qt.shape[0]
    lo = _half_mask(n)
    zero = jnp.zeros((n, LANES), qt.dtype)
    parts = []
    for kvh in range(N_KV_HEADS):
        for g in range(GROUP):
            blk = qt[:, g * LANES:(g + 1) * LANES]
            parts.append(jnp.where(lo if kvh == 0 else ~lo, blk, zero))
    return jnp.concatenate(parts, axis=0)


def _unstack_o(o, n):
    lo = _half_mask(n)
    cols = [jnp.where(lo, o[g * n:(g + 1) * n, :], o[(GROUP + g) * n:(GROUP + g + 1) * n, :]) for g in range(GROUP)]
    return jnp.concatenate(cols, axis=1)


def _masked_softmax(s, mask):
    m = jnp.max(jnp.where(mask, s, NEG_INF), axis=-1, keepdims=True)
    e = jnp.where(mask, jnp.exp2(s - m), 0.0)
    l = jnp.sum(e, axis=-1, keepdims=True)
    return e / jnp.where(l > 0.0, l, 1.0)


def _add_tile_bias(s, bias):
    n = bias.shape[0]
    return jnp.concatenate([s[r * n:(r + 1) * n, :] + bias for r in range(s.shape[0] // n)], axis=0)


def _split_bf16(x, terms):
    out = []
    for _ in range(terms):
        hi = x.astype(BF16)
        out.append(hi)
        x = x - hi.astype(F32)
    return out


def _dot_exact_rhs(x, rhs_bf16, terms):
    return sum(_dot(t, rhs_bf16) for t in _split_bf16(x, terms))


def _combine(gates, expand, o_c, o_s, o_w, g_out):
    gx = _dot_exact_rhs(gates, expand, 2)
    ya = gx[:, :ATTN_DIM] * o_c + gx[:, ATTN_DIM:2 * ATTN_DIM] * o_s + gx[:, 2 * ATTN_DIM:] * o_w
    return _rms(ya, g_out).astype(BF16)


def _cmp_to_sel(nc_rows, ns):
    n = np.arange(nc_rows)[:, None]
    s = np.arange(LANES)[None, :]
    ov = (n * CMP_STRIDE <= s * SEL_BLOCK + SEL_BLOCK - 1) & (n * CMP_STRIDE + CMP_LEN - 1 >= s * SEL_BLOCK) & (s < ns)
    return jnp.asarray(ov.astype(np.float32), dtype=BF16)


def _gate_expand():
    e = np.zeros((GATE_PAD, 3 * ATTN_DIM), np.float32)
    for br in range(3):
        for kvh in range(N_KV_HEADS):
            for g in range(GROUP):
                c0 = br * ATTN_DIM + g * LANES + kvh * HEAD_DIM
                e[br * N_HEADS + kvh * GROUP + g, c0:c0 + HEAD_DIM] = 1.0
    return jnp.asarray(e, dtype=BF16)


def _sel_onehot(n_keys):
    t = np.arange(n_keys)[:, None] // SEL_BLOCK
    return jnp.asarray((t == np.arange(LANES)[None, :]).astype(np.float32), dtype=BF16)


def _chunk_perm():
    m = np.arange(PAGE_SIZE)[:, None]
    t = np.arange(PAGE_SIZE)[None, :]
    per_page = PAGE_SIZE // CMP_STRIDE
    return jnp.asarray((t == CMP_STRIDE * (m % per_page) + m // per_page).astype(np.float32), dtype=BF16)


def _rank_penalty(imp_t, ns):
    r = imp_t.shape[1]
    sub = lax.broadcasted_iota(jnp.int32, (8, r), 0)
    cnt = jnp.zeros((ns, r), F32)
    for sp in range(ns):
        other = imp_t[sp:sp + 1, :]
        parts = []
        for blk in range(ns // 8):
            t = imp_t[blk * 8:(blk + 1) * 8, :]
            if blk * 8 > sp:
                parts.append(jnp.where(other >= t, 1.0, 0.0))
            elif blk * 8 + 7 <= sp:
                parts.append(jnp.where(other > t, 1.0, 0.0))
            else:
                parts.append(jnp.where(sub > sp - blk * 8, jnp.where(other >= t, 1.0, 0.0), jnp.where(other > t, 1.0, 0.0)))
        cnt = cnt + jnp.concatenate(parts, axis=0)
    return jnp.where(cnt < N_SEL, 0.0, NEG_INF)


def _prompt_attn_body(ns, q_ref, kc_ref, vc_ref, ksl_ref, vsl_ref, kw_ref, vw_ref, oh_ref, ov_ref, gates_ref,
                      ex_ref, go_ref, out_ref, pen_s):
    i = pl.program_id(1)
    nq = Q_BLOCK
    rows = N_HEADS * nq
    qz = _stack_q(q_ref[...])
    qpos_t = i * nq + lax.broadcasted_iota(jnp.int32, (nq, 1), 0)

    kc = kc_ref[...]
    nc = kc.shape[0]
    c_end = lax.broadcasted_iota(jnp.int32, (nq, nc), 1) * CMP_STRIDE + (CMP_LEN - 1)
    s_c = _add_tile_bias(_halves(_dot_nt, qz, kc), jnp.where(c_end <= qpos_t, 0.0, NEG_INF))
    e_c = jnp.exp2(s_c - jnp.max(s_c, axis=-1, keepdims=True))
    any_c = jnp.where(qpos_t >= CMP_LEN - 1, 1.0, 0.0)
    p_c = e_c * (jnp.concatenate([any_c] * N_HEADS, axis=0) / jnp.sum(e_c, axis=-1, keepdims=True))
    o_c = _dot(p_c.astype(BF16), vc_ref[...])

    psum = jnp.concatenate(
        [sum(p_c[(kvh * GROUP + g) * nq:(kvh * GROUP + g + 1) * nq, :] for g in range(GROUP))
         for kvh in range(N_KV_HEADS)], axis=0)
    imp = _dot_exact_rhs(psum, ov_ref[...], 3)
    r2 = N_KV_HEADS * nq
    s_idx = lax.broadcasted_iota(jnp.int32, (r2, LANES), 1)
    q_blk = (i * nq + lax.broadcasted_iota(jnp.int32, (r2, LANES), 0) % nq) // SEL_BLOCK
    forced = (s_idx == 0) | (s_idx == q_blk) | (s_idx == q_blk - 1)
    imp = jnp.where(s_idx <= q_blk, jnp.where(forced, FORCE_SCORE, imp), -1.0)
    imp = jnp.where(s_idx < ns, imp, -2.0)
    pen_s[...] = jnp.zeros((LANES, r2), F32)

    @pl.when((i + 1) * nq > N_SEL * SEL_BLOCK)
    def _():
        pen_s[0:ns, :] = _rank_penalty(imp.T[0:ns, :], ns)

    pen = pen_s[...].T
    pen = jnp.where(s_idx <= q_blk, pen, jnp.where(s_idx < ns, NEG_INF, 0.0)).astype(BF16)
    pen_rows = jnp.concatenate([pen[kvh * nq:(kvh + 1) * nq, :] for kvh in range(N_KV_HEADS) for _ in range(GROUP)],
                               axis=0)
    q_aug = jnp.concatenate([qz, pen_rows], axis=1)

    ones_v = jnp.ones((KEY_CHUNK, LANES), BF16)

    def scores(c):
        k0 = pl.multiple_of(c * KEY_CHUNK, KEY_CHUNK)
        k_aug = jnp.concatenate([ksl_ref[pl.ds(k0, KEY_CHUNK), :], oh_ref[pl.ds(k0, KEY_CHUNK), :]], axis=1)
        return _halves(_dot_nt, q_aug, k_aug).astype(BF16)

    def update(c, s, m_old, acc):
        k0 = pl.multiple_of(c * KEY_CHUNK, KEY_CHUNK)
        m_new = jnp.maximum(m_old, jnp.max(s, axis=-1, keepdims=True).astype(F32))
        p = jnp.exp2(s - m_new.astype(BF16))
        v_aug = jnp.concatenate([vsl_ref[pl.ds(k0, KEY_CHUNK), :], ones_v], axis=1)
        return m_new, jnp.exp2(m_old - m_new) * acc + _dot(p, v_aug)

    last = (i * nq) // KEY_CHUNK
    carry = (jnp.full((rows, 1), NEG_INF, F32), jnp.zeros((rows, 2 * LANES), F32))
    m_old, acc = lax.fori_loop(0, last, lambda c, cr: update(c, scores(c), *cr), carry)
    kpos = last * KEY_CHUNK + lax.broadcasted_iota(jnp.int32, (nq, KEY_CHUNK), 1)
    _, acc = update(last, _add_tile_bias(scores(last), jnp.where(kpos <= qpos_t, 0.0, NEG_INF).astype(BF16)), m_old, acc)
    o_s = acc[:, :LANES] / acc[:, LANES:]

    nw = WINDOW + nq
    w0 = pl.multiple_of(jnp.maximum(i * nq - WINDOW, 0), nq)
    diff = qpos_t - (w0 + lax.broadcasted_iota(jnp.int32, (nq, nw), 1))
    s_w = _add_tile_bias(_halves(_dot_nt, qz, kw_ref[pl.ds(w0, nw), :]).astype(BF16),
                         jnp.where((diff >= 0) & (diff < WINDOW), 0.0, NEG_INF).astype(BF16))
    e_w = jnp.exp2(s_w - jnp.max(s_w, axis=-1, keepdims=True))
    ow_aug = _dot(e_w, jnp.concatenate([vw_ref[pl.ds(w0, nw), :], jnp.ones((nw, LANES), BF16)], axis=1))
    o_w = ow_aug[:, :LANES] / ow_aug[:, LANES:]

    out_ref[...] = _combine(gates_ref[...], ex_ref[...], _unstack_o(o_c, nq), _unstack_o(o_s, nq),
                            _unstack_o(o_w, nq), go_ref[...])


def _prompt_attn(q, kc, vc, kvb, gates, layer, w, consts):
    nb, s, _ = q.shape
    nc = kc.shape[1]
    ns = s // SEL_BLOCK
    col = lambda c: pl.BlockSpec((None, s, LANES), lambda b, i: (b, 0, c))
    whole = lambda a: pl.BlockSpec(a.shape, lambda b, i: (0,) * a.ndim)
    return pl.pallas_call(
        functools.partial(_prompt_attn_body, ns),
        grid=(nb, s // Q_BLOCK),
        in_specs=[pl.BlockSpec((None, Q_BLOCK, ATTN_DIM), lambda b, i: (b, i, 0)),
                  pl.BlockSpec((None, nc, LANES), lambda b, i: (b, 0, 0)),
                  pl.BlockSpec((None, nc, LANES), lambda b, i: (b, 0, 0)),
                  col(0), col(1), col(2), col(3),
                  whole(consts["onehot"]), whole(consts["ov"]),
                  pl.BlockSpec((None, Q_BLOCK, GATE_PAD), lambda b, i: (b, i, 0)),
                  whole(consts["expand"]),
                  pl.BlockSpec((None, 1, ATTN_DIM), lambda b, i: (layer, 0, 0))],
        out_specs=pl.BlockSpec((None, Q_BLOCK, ATTN_DIM), lambda b, i: (b, i, 0)),
        out_shape=jax.ShapeDtypeStruct((nb, s, ATTN_DIM), BF16),
        scratch_shapes=[pltpu.VMEM((LANES, N_KV_HEADS * Q_BLOCK), F32)],
        compiler_params=_params("arbitrary", "arbitrary"), name="prompt_attn",
    )(q, kc, vc, kvb, kvb, kvb, kvb, consts["onehot"], consts["ov"], gates, consts["expand"], w["g_attn_out"])


def _sample_attn_body(n_pages, nseq, pt_ref, q_ref, kvp_ref, kvw_ref, gates_ref, *rest):
    n_pg = nseq * n_pages
    pages = rest[:n_pg]
    (win_ref, wk_ref, wv_ref, bk_ref, bv_ref, w2k_ref, w2v_ref, perm_ref, oh_ref, ov_ref, ex_ref,
     go_ref) = rest[n_pg:n_pg + 12]
    out_ref, wout_ref, sbuf = rest[-3:]
    past = n_pages * PAGE_SIZE
    qpos = past
    nch = past // CMP_STRIDE
    per_page = PAGE_SIZE // CMP_STRIDE
    seqs = range(nseq)
    R = nseq * N_HEADS
    perm = perm_ref[...]

    seq_pages = [pages[sq * n_pages:(sq + 1) * n_pages] for sq in seqs]
    xs = []
    for pgs in seq_pages:
        x_all = _dot_nt(perm, jnp.concatenate([pg[0:2 * LANES, :] for pg in pgs], axis=0).astype(BF16))
        xs += [x_all[:, p * 2 * LANES:(p + 1) * 2 * LANES] for p in range(n_pages)]

    def compress(c0, w1, b, w2):
        acc = jnp.zeros((nseq * nch, 2 * LANES), F32)
        for t in range(CMP_STRIDE // 2):
            lhs = jnp.concatenate(
                [jnp.concatenate([x[j * per_page:(j + 1) * per_page, c0:c0 + LANES] for j in (2 * t, 2 * t + 1)], axis=1)
                 for x in xs], axis=0)
            acc = acc + _dot(lhs.astype(BF16), w1[t])
        c = _cmp_finish(acc, b[...], w2[...], sbuf).astype(BF16)
        return [c[sq * nch:(sq + 1) * nch, :] for sq in seqs]

    kcs = compress(0, wk_ref, bk_ref, w2k_ref)
    vcs = compress(LANES, wv_ref, bv_ref, w2v_ref)

    ns = (past + 1 + SEL_BLOCK - 1) // SEL_BLOCK
    q_blk = qpos // SEL_BLOCK
    w_len = win_ref.shape[2]
    lane = lax.broadcasted_iota(jnp.int32, (R, LANES), 1)
    rows_of = lambda x, sq: x[sq * N_HEADS:(sq + 1) * N_HEADS, :]
    per_row = lambda x: jnp.concatenate([jnp.broadcast_to(x[sq:sq + 1, :], (N_HEADS, x.shape[1])) for sq in seqs], axis=0)
    as_bf16 = lambda x: x.astype(BF16).astype(F32)
    qzs = [_stack_q(q_ref[sq:sq + 1, :]) for sq in seqs]
    qz = jnp.concatenate(qzs, axis=0)
    qf = qz.astype(F32)
    kv_new = per_row(kvp_ref[...])
    kvw_new = per_row(kvw_ref[...])

    s_c = jnp.concatenate([_dot_nt(qzs[sq], kcs[sq]) for sq in seqs], axis=0)
    blk = lax.broadcasted_iota(jnp.int32, (R, nch), 1)
    p_c = _masked_softmax(s_c, (blk * CMP_STRIDE + (CMP_LEN - 1) <= qpos) & (blk < nch - 1))
    p_cb = p_c.astype(BF16)
    o_c = jnp.concatenate([_dot(rows_of(p_cb, sq), vcs[sq]) for sq in seqs], axis=0)

    psum = jnp.concatenate(
        [jnp.sum(p_c[sq * N_HEADS + kvh * GROUP:sq * N_HEADS + (kvh + 1) * GROUP, :], axis=0, keepdims=True)
         if kvh < N_KV_HEADS else jnp.zeros((N_HEADS - N_KV_HEADS, nch), F32)
         for sq in seqs for kvh in range(N_KV_HEADS + 1)], axis=0)
    imp = _dot_exact_rhs(psum, ov_ref[...], 3)
    forced = (lane == 0) | (lane == q_blk) | (lane == q_blk - 1)
    imp = jnp.where(lane <= q_blk, jnp.where(forced, FORCE_SCORE, imp), -1.0)
    imp = jnp.where(lane < ns, imp, -2.0)
    imp_t = imp.T
    sub = lax.broadcasted_iota(jnp.int32, (LANES, LANES), 0)
    lan = lax.broadcasted_iota(jnp.int32, (LANES, LANES), 1)
    pens = []
    for sq in seqs:
        for kvh in range(N_KV_HEADS):
            r = sq * N_HEADS + kvh
            other = imp_t[:, r:r + 1]
            mine = imp[r:r + 1, :]
            beats = (other > mine) | ((other == mine) & (sub < lan))
            cnt = jnp.sum(beats.astype(jnp.int32), axis=0, keepdims=True)
            sel = (cnt < N_SEL) & (lane[0:1, :] <= q_blk)
            pens += [jnp.where(sel | (lane[0:1, :] >= ns), 0.0, NEG_INF)] * GROUP
    pen = jnp.concatenate(pens, axis=0)
    q_aug = jnp.concatenate([qz, pen.astype(BF16)], axis=1)

    tokens = lambda pgs, r0: jnp.concatenate([pg[r0:r0 + LANES, :] for pg in pgs], axis=1).astype(BF16)
    s_s = jnp.concatenate(
        [_dot(rows_of(q_aug, sq), jnp.concatenate([tokens(seq_pages[sq], 2 * LANES), oh_ref[...]], axis=0))
         for sq in seqs], axis=0)
    pen_new = jnp.sum(jnp.where(lane == q_blk, pen, 0.0), axis=-1, keepdims=True)
    s_new = jnp.sum(qf * as_bf16(kv_new[:, 2 * LANES:3 * LANES]), axis=-1, keepdims=True) + pen_new
    m = jnp.maximum(jnp.max(s_s, axis=-1, keepdims=True), s_new)
    p_s = jnp.exp2(s_s - m)
    p_new = jnp.exp2(s_new - m)
    den = jnp.sum(p_s, axis=-1, keepdims=True) + p_new
    p_sb = p_s.astype(BF16)
    pv = [_dot_nt(rows_of(p_sb, sq), tokens(seq_pages[sq], 3 * LANES)) for sq in seqs]
    o_s = (jnp.concatenate(pv, axis=0) + as_bf16(p_new) * as_bf16(kv_new[:, 3 * LANES:4 * LANES])) / den

    wins = [win_ref[sq] for sq in seqs]
    s_w = jnp.concatenate([_dot(qzs[sq], wins[sq][:LANES, :].astype(BF16)) for sq in seqs], axis=0)
    diff = qpos - (past - w_len + lax.broadcasted_iota(jnp.int32, (R, w_len), 1))
    ok = (diff >= 0) & (diff < WINDOW)
    s_wn = jnp.sum(qf * as_bf16(kvw_new[:, :LANES]), axis=-1, keepdims=True)
    m = jnp.maximum(jnp.max(jnp.where(ok, s_w, NEG_INF), axis=-1, keepdims=True), s_wn)
    p_w = jnp.where(ok, jnp.exp2(s_w - m), 0.0)
    p_wn = jnp.exp2(s_wn - m)
    den = jnp.sum(p_w, axis=-1, keepdims=True) + p_wn
    p_wb = p_w.astype(BF16)
    o_w = (jnp.concatenate([_dot_nt(rows_of(p_wb, sq), wins[sq][LANES:, :].astype(BF16)) for sq in seqs], axis=0)
           + as_bf16(p_wn) * as_bf16(kvw_new[:, LANES:])) / den

    o_c, o_s, o_w = (jnp.concatenate([_unstack_o(rows_of(o, sq), 1) for sq in seqs], axis=0) for o in (o_c, o_s, o_w))
    out_ref[...] = _combine(gates_ref[...], ex_ref[...], o_c, o_s, o_w, go_ref[...])

    eye_w = (lax.broadcasted_iota(jnp.int32, (2 * LANES, 2 * LANES), 0)
             == lax.broadcasted_iota(jnp.int32, (2 * LANES, 2 * LANES), 1))
    last_lane = lax.broadcasted_iota(jnp.int32, (2 * LANES, w_len), 1) == w_len - 1
    for sq in seqs:
        new_col = jnp.sum(jnp.where(eye_w, kvw_ref[sq:sq + 1, :], 0.0), axis=1, keepdims=True)
        wout_ref[sq] = jnp.where(last_lane, new_col, pltpu.roll(wins[sq], w_len - 1, 1))


def _sample_attn(q, kvp, kvw, gates, cache_kv_t, cache_win_t, page_table, layer, w, consts, win_all=None, nseq=4):
    db, n_pages = page_table.shape
    nl, w_len = cache_win_t.shape[0], cache_win_t.shape[3]
    nch = n_pages * PAGE_SIZE // CMP_STRIDE
    assert db % nseq == 0
    r3 = lambda a: a.reshape(db // nseq, nseq, a.shape[-1])
    rowspec = lambda c: pl.BlockSpec((None, nseq, c), lambda b, pt: (b, 0, 0))
    whole = lambda a: pl.BlockSpec(a.shape, lambda b, pt: (0,) * a.ndim)
    wspec = lambda a: pl.BlockSpec((None,) + a.shape[1:], lambda b, pt: (layer,) + (0,) * (a.ndim - 1))
    page_specs = [pl.BlockSpec((None, None, N_PAGED * LANES, PAGE_SIZE),
                               functools.partial(lambda col, b, pt: (layer, pt[b, col], 0, 0), sq * n_pages + p))
                  for sq in range(nseq) for p in range(n_pages)]
    ws = [w["cmp_w1_k"], w["cmp_w1_v"], w["cmp_b_k"], w["cmp_b_v"], w["cmp_w2_k"], w["cmp_w2_v"]]
    cs = [consts["perm"], consts["onehot_s"], consts["ov_s"], consts["expand"]]
    win_spec = pl.BlockSpec((None, nseq, 2 * LANES, w_len), lambda b, pt: (layer, b, 0, 0))
    args = [page_table.reshape(db // nseq, nseq * n_pages), r3(q), r3(kvp), r3(kvw), r3(gates), *([cache_kv_t] * (nseq * n_pages)), cache_win_t, *ws, *cs,
            w["g_attn_out"]]
    in_specs = ([rowspec(ATTN_DIM), rowspec(N_PAGED * LANES), rowspec(2 * LANES), rowspec(GATE_PAD)] + page_specs
                + [win_spec] + [wspec(a) for a in ws] + [whole(a) for a in cs] + [wspec(w["g_attn_out"])])
    aliases = {}
    if win_all is not None:
        in_specs.append(pl.BlockSpec(memory_space=pl.ANY))
        args.append(win_all)
        aliases = {len(args) - 1: 1}
    grid_spec = pltpu.PrefetchScalarGridSpec(
        num_scalar_prefetch=1, grid=(db // nseq,), in_specs=in_specs, out_specs=[rowspec(ATTN_DIM), win_spec],
        scratch_shapes=[pltpu.VMEM((nseq * nch + 8, LANES), F32)],
    )
    out, win_all = pl.pallas_call(
        functools.partial(_sample_attn_body, n_pages, nseq),
        grid_spec=grid_spec,
        out_shape=[jax.ShapeDtypeStruct((db // nseq, nseq, ATTN_DIM), BF16),
                   jax.ShapeDtypeStruct((nl, db, 2 * LANES, w_len), F32)],
        input_output_aliases=aliases,
        compiler_params=_params("arbitrary"), name="sample_attn",
    )(*args)
    return out.reshape(db, ATTN_DIM), win_all


def _postmix_body(final, x_ref, yc_ref, ya_ref, woc_ref, woa_ref, gf_ref, wg_ref, wu_ref, wd_ref, gl_ref, xo_ref, *rest):
    h_s, acc_s = rest[-2:]
    f = pl.program_id(1)

    @pl.when(f == 0)
    def _():
        x1 = x_ref[...] + _dot(yc_ref[...], woc_ref[...]) + _dot(ya_ref[...], woa_ref[...])
        h_s[...] = _rms(x1, gf_ref[...]).astype(BF16)
        acc_s[...] = x1

    h = h_s[...]
    a = _dot(h, wg_ref[...])
    act = (a * jax.nn.sigmoid(a) * _dot(h, wu_ref[...])).astype(BF16)
    acc_s[...] += _dot(act, wd_ref[...])

    @pl.when(f == pl.num_programs(1) - 1)
    def _():
        out = acc_s[...]
        xo_ref[...] = out
        if final:
            rest[0][...] = _rms(out, gl_ref[...])


def _postmix(x, ycn, yan, layer, w, final, tm=1024, n_ff=11):
    t, d = x.shape
    tm = min(tm, t)
    d_ff = w["w_gate"].shape[-1]
    tf = d_ff // n_ff
    assert tf * n_ff == d_ff and tf % LANES == 0
    row = lambda c: pl.BlockSpec((tm, c), lambda i, f: (i, 0))
    wspec = lambda a: pl.BlockSpec((None,) + a.shape[1:], lambda i, f: (layer,) + (0,) * (a.ndim - 1))
    gl = w["g_final"]
    n_out = 2 if final else 1
    outs = pl.pallas_call(
        functools.partial(_postmix_body, final),
        grid=(t // tm, n_ff),
        in_specs=[row(d), row(CONV_DIM), row(ATTN_DIM), wspec(w["w_out_c"]), wspec(w["w_out_a"]), wspec(w["g_ffn"]),
                  pl.BlockSpec((None, d, tf), lambda i, f: (layer, 0, f)),
                  pl.BlockSpec((None, d, tf), lambda i, f: (layer, 0, f)),
                  pl.BlockSpec((None, tf, d), lambda i, f: (layer, f, 0)),
                  pl.BlockSpec(gl.shape, lambda i, f: (0, 0))],
        out_specs=[row(d)] * n_out,
        out_shape=[jax.ShapeDtypeStruct((t, d), F32)] * n_out,
        scratch_shapes=[pltpu.VMEM((tm, d), BF16), pltpu.VMEM((tm, d), F32)],
        compiler_params=_params("arbitrary", "arbitrary"), name="postmix",
    )(x, ycn, yan, w["w_out_c"], w["w_out_a"], w["g_ffn"], w["w_gate"], w["w_up"], w["w_down"], gl)
    return outs


def _prep_weights(g_mix, w_in, conv_w, cmp_pe_k, cmp_w1_k, cmp_w2_k, cmp_pe_v, cmp_w1_v, cmp_w2_v, g_conv_out,
                  g_attn_out, w_out, g_ffn, w_gate, w_up, w_down, g_final):
    nl, d, _ = w_in.shape
    c3 = 3 * CONV_DIM
    kv_cols = 6 * LANES

    def perm_heads(a, axis):
        shp = a.shape
        a = a.reshape(shp[:axis] + (N_KV_HEADS, GROUP, HEAD_DIM) + shp[axis + 1:])
        a = jnp.swapaxes(a, axis, axis + 1)
        return a.reshape(shp)

    def blockdiag2(a):
        z = jnp.zeros_like(a)
        return jnp.concatenate([jnp.concatenate([a, z], axis=-1), jnp.concatenate([z, a], axis=-1)], axis=-2)

    def cmp_w1(w1):
        a = blockdiag2(w1[:, :CMP_STRIDE])
        b = blockdiag2(w1[:, CMP_STRIDE:])
        return jnp.concatenate([a, b], axis=-1).astype(BF16).reshape(nl, CMP_STRIDE // 2, 2 * LANES, 2 * LANES)

    pe = jnp.stack([cmp_pe_k, cmp_pe_v], axis=1).reshape(nl, 2, CMP_LEN * HEAD_DIM, 1)
    w1 = jnp.stack([cmp_w1_k, cmp_w1_v], axis=1).reshape(nl, 2, CMP_LEN * HEAD_DIM, HEAD_DIM)
    bias = _cmp_bias(pe, w1)
    bias2 = jnp.concatenate([bias, bias], axis=-1)
    gates_w = jnp.pad(w_in[:, :, c3 + ATTN_DIM + kv_cols:], ((0, 0), (0, 0), (0, GATE_PAD - 3 * N_HEADS)))
    return {
        "g_mix": g_mix.reshape(nl, 1, d),
        "w_c": w_in[:, :, :c3].astype(BF16),
        "w_q": perm_heads(w_in[:, :, c3:c3 + ATTN_DIM], 2).astype(BF16),
        "w_kv": w_in[:, :, c3 + ATTN_DIM:c3 + ATTN_DIM + kv_cols].astype(BF16),
        "w_g": gates_w.astype(BF16),
        "conv_w": conv_w,
        "g_conv_out": g_conv_out.reshape(nl, 1, CONV_DIM),
        "cmp_w1_k": cmp_w1(cmp_w1_k), "cmp_w1_v": cmp_w1(cmp_w1_v),
        "cmp_b_k": bias2[:, 0], "cmp_b_v": bias2[:, 1],
        "cmp_w2_k": blockdiag2(cmp_w2_k).astype(BF16), "cmp_w2_v": blockdiag2(cmp_w2_v).astype(BF16),
        "g_attn_out": perm_heads(g_attn_out, 1).reshape(nl, 1, ATTN_DIM),
        "w_out_c": w_out[:, :CONV_DIM].astype(BF16),
        "w_out_a": perm_heads(w_out[:, CONV_DIM:], 1).astype(BF16),
        "g_ffn": g_ffn.reshape(nl, 1, d),
        "w_gate": w_gate.astype(BF16), "w_up": w_up.astype(BF16), "w_down": w_down.astype(BF16),
        "g_final": g_final.reshape(1, d),
    }


def kernel(x_prompt, x_sample, cache_kv, cache_win, state_conv, page_table, g_mix, w_in, conv_w, cmp_pe_k, cmp_w1_k, cmp_w2_k, cmp_pe_v, cmp_w1_v, cmp_w2_v, g_conv_out, g_attn_out, w_out, g_ffn, w_gate, w_up, w_down, g_final):
    nl = w_in.shape[0]
    b, s, d = x_prompt.shape
    db, sq = x_sample.shape[:2]
    assert sq == 1 and s % 512 == 0
    n_pages = page_table.shape[1]
    past = n_pages * PAGE_SIZE
    w_len = cache_win.shape[2]
    w_keep = min(WINDOW, s)
    w = _prep_weights(g_mix, w_in, conv_w, cmp_pe_k, cmp_w1_k, cmp_w2_k, cmp_pe_v, cmp_w1_v, cmp_w2_v, g_conv_out,
                      g_attn_out, w_out, g_ffn, w_gate, w_up, w_down, g_final)
    consts = {
        "expand": _gate_expand(),
        "onehot": _sel_onehot(s), "ov": _cmp_to_sel(s // CMP_STRIDE, s // SEL_BLOCK),
        "onehot_s": _sel_onehot(past).T, "ov_s": _cmp_to_sel(past // CMP_STRIDE, (past + 1 + SEL_BLOCK - 1) // SEL_BLOCK),
        "perm": _chunk_perm(),
    }
    cache_kv = cache_kv.transpose(0, 1, 3, 4, 5, 2).reshape(nl, cache_kv.shape[1], N_PAGED * LANES, PAGE_SIZE)
    cache_win = cache_win.transpose(0, 1, 3, 4, 5, 2).reshape(nl, db, 2 * LANES, w_len)

    xp, xs = x_prompt, x_sample
    kvp_l, winp_l, convp_l, kvs_l, convs_l = [], [], [], [], []
    win_all = None
    for l in range(nl):
        final = l == nl - 1
        ycn, q, kvp, kvw, kvb, gates, tail = _premix(xp, l, w)
        kc, vc = _compress(kvp, l, w)
        yan = _prompt_attn(q, kc, vc, kvb, gates, l, w, consts)
        outs = _postmix(xp.reshape(b * s, d), ycn.reshape(b * s, -1), yan.reshape(b * s, -1), l, w, final)
        xp = outs[0].reshape(b, s, d)
        if final:
            y_prompt = outs[1].reshape(b, s, d)
        kvp_l.append(kvp)
        winp_l.append(kvw[:, s - w_keep:])
        convp_l.append(tail[:, 8 - (CONV_WIDTH - 1):])
        ycn, q, kvp, kvw, kvb, gates, u = _premix(xs, l, w, prev=(state_conv[l, :, 1], state_conv[l, :, 0]))
        yan, win_all = _sample_attn(q, kvp, kvw, gates, cache_kv, cache_win, page_table, l, w, consts, win_all)
        outs = _postmix(xs.reshape(db, d), ycn, yan, l, w, final)
        xs = outs[0].reshape(db, 1, d)
        if final:
            y_sample = outs[1].reshape(db, 1, d)
        kvs_l.append(kvp)
        convs_l.append(jnp.stack([state_conv[l, :, 1], u], axis=1))
    kv_shape = (N_PAGED, N_KV_HEADS, HEAD_DIM)
    return (y_prompt, y_sample,
            jnp.stack(kvp_l).reshape((nl, b, s) + kv_shape),
            jnp.stack(winp_l).reshape((nl, b, w_keep, 2) + kv_shape[1:]),
            jnp.stack(convp_l),
            jnp.stack(kvs_l).reshape((nl, db, 1) + kv_shape),
            win_all.reshape((nl, db, 2) + kv_shape[1:] + (w_len,)).transpose(0, 1, 5, 2, 3, 4),
            jnp.stack(convs_l))
```
